```python
import math
import jax, jax.numpy as jnp
from jax import lax
import numpy as np

D_MODEL = 1024
BATCH = 8
SEQ = 2048
DEPTH = 4

HEAD_DIM = 64
DIL_GROUPS = ((128, 1), (512, 4), (2048, 16))
N_DIL = len(DIL_GROUPS)
HEADS_PER_DIL = 4
HA = N_DIL * HEADS_PER_DIL
HB = 6
HC = 6
N_HEADS = HA + HB + HC
MIX_WIDTH = N_HEADS * HEAD_DIM
BAND_BLOCK = 128
MOBA_BLOCK = 256
MOBA_TOPK = 3
MOBA_CHUNK = 32
SB_BLOCK = 128
N_BRANCH = 3
D_FF = -(-8 * D_MODEL // (3 * 256)) * 256
ROPE_THETA = 10000.0
NORM_EPS = 1e-6

kernel_name = "hybrid_gated_dilated_moba_stickbreaking_block"


def rms_norm(x, g):
    xf = x.astype(jnp.float32)
    y = xf * lax.rsqrt(jnp.mean(xf * xf, axis=-1, keepdims=True) + NORM_EPS)
    return (y * g.astype(jnp.float32)).astype(x.dtype)


def rope_tables(seq):
    pos = jnp.arange(seq, dtype=jnp.float32)
    inv = ROPE_THETA ** (-jnp.arange(0, HEAD_DIM, 2, dtype=jnp.float32) / HEAD_DIM)
    ang = pos[:, None] * inv[None, :]
    return jnp.cos(ang), jnp.sin(ang)


def apply_rope(x, cos, sin):
    x1, x2 = jnp.split(x, 2, axis=-1)
    cos = cos.astype(x.dtype)
    sin = sin.astype(x.dtype)
    return jnp.concatenate([x1 * cos - x2 * sin, x1 * sin + x2 * cos], axis=-1)


def _softmax_lse(s):
    mx = jnp.max(s, axis=-1, keepdims=True)
    e = jnp.exp(s - mx)
    den = jnp.sum(e, axis=-1, keepdims=True)
    return e / den, (mx + jnp.log(den))[..., 0]


def dilated_window_attention(q, k, v, window, dilation):
    B, H, S, Dh = q.shape
    L = S // dilation
    span = window // dilation
    blk = BAND_BLOCK
    nb = -(-L // blk)
    pad_r = nb * blk - L

    def to_sub(a):
        return a.reshape(B, H, L, dilation, Dh).transpose(0, 1, 3, 2, 4)

    qs, ks, vs = to_sub(q), to_sub(k), to_sub(v)
    qs = jnp.pad(qs, ((0, 0), (0, 0), (0, 0), (0, pad_r), (0, 0)))
    ks = jnp.pad(ks, ((0, 0), (0, 0), (0, 0), (blk, pad_r), (0, 0)))
    vs = jnp.pad(vs, ((0, 0), (0, 0), (0, 0), (blk, pad_r), (0, 0)))
    qb = qs.reshape(B, H, dilation, nb, blk, Dh)
    kb = ks.reshape(B, H, dilation, nb + 1, blk, Dh)
    vb = vs.reshape(B, H, dilation, nb + 1, blk, Dh)
    kw = jnp.concatenate([kb[:, :, :, :-1], kb[:, :, :, 1:]], axis=4)
    vw = jnp.concatenate([vb[:, :, :, :-1], vb[:, :, :, 1:]], axis=4)
    m = jnp.arange(nb)[:, None] * blk + jnp.arange(blk)[None, :]
    n = jnp.arange(nb)[:, None] * blk - blk + jnp.arange(2 * blk)[None, :]
    rel = m[:, :, None] - n[:, None, :]
    mask = (rel >= 0) & (rel <= span) & (n[:, None, :] >= 0)
    s = jnp.einsum("bhrnqd,bhrnkd->bhrnqk", qb, kw).astype(jnp.float32) * (HEAD_DIM ** -0.5)
    s = jnp.where(mask, s, -jnp.inf)
    p, lse = _softmax_lse(s)
    o = jnp.einsum("bhrnqk,bhrnkd->bhrnqd", p.astype(v.dtype), vw)
    o = o.reshape(B, H, dilation, nb * blk, Dh)[:, :, :, :L]
    o = o.transpose(0, 1, 3, 2, 4).reshape(B, H, S, Dh)
    lse = lse.reshape(B, H, dilation, nb * blk)[:, :, :, :L]
    lse = lse.transpose(0, 1, 3, 2).reshape(B, H, S)
    return o, lse


def dilated_mixture(qa, ka, va):
    outs, lses = [], []
    for g, (window, dilation) in enumerate(DIL_GROUPS):
        sl = slice(g * HEADS_PER_DIL, (g + 1) * HEADS_PER_DIL)
        o, lse = dilated_window_attention(qa[:, sl], ka[:, sl], va[:, sl], window, dilation)
        outs.append(o)
        lses.append(lse)
    w = jax.nn.softmax(jnp.stack(lses, axis=0), axis=0)
    o = jnp.sum(w[..., None] * jnp.stack(outs, axis=0).astype(jnp.float32), axis=0)
    return o.astype(qa.dtype)


def moba_attention(q, k, v):
    B, H, S, Dh = q.shape
    nblk = -(-S // MOBA_BLOCK)
    pad = nblk * MOBA_BLOCK - S
    kb = jnp.pad(k, ((0, 0), (0, 0), (0, pad), (0, 0))).reshape(B, H, nblk, MOBA_BLOCK, Dh)
    vb = jnp.pad(v, ((0, 0), (0, 0), (0, pad), (0, 0))).reshape(B, H, nblk, MOBA_BLOCK, Dh)
    topk = max(1, min(MOBA_TOPK, nblk - 1))
    own_blk = jnp.arange(S) // MOBA_BLOCK
    k_mean = jnp.mean(kb.astype(jnp.float32), axis=3)
    gate = jnp.einsum("bhsd,bhnd->bhsn", q.astype(jnp.float32), k_mean)
    past = jnp.arange(nblk)[None, :] < own_blk[:, None]
    gate = jnp.where(past, gate, -jnp.inf)
    _, sel = lax.top_k(gate, topk)
    valid = sel < own_blk[:, None]
    nq = S // MOBA_CHUNK
    scale = HEAD_DIM ** -0.5
    bi = jnp.arange(B)[:, None, None, None]
    hi = jnp.arange(H)[None, :, None, None]

    def to_chunks(a):
        return jnp.moveaxis(a.reshape(B, H, nq, MOBA_CHUNK, *a.shape[3:]), 2, 0)

    def chunk(args):
        i, qi, si, vi = args
        t = i * MOBA_CHUNK + jnp.arange(MOBA_CHUNK)
        ob = (i * MOBA_CHUNK) // MOBA_BLOCK
        k_own = lax.dynamic_index_in_dim(kb, ob, axis=2, keepdims=False)
        v_own = lax.dynamic_index_in_dim(vb, ob, axis=2, keepdims=False)
        k_sel = kb[bi, hi, si]
        v_sel = vb[bi, hi, si]
        s_sel = jnp.einsum("bhqd,bhqkld->bhqkl", qi, k_sel).astype(jnp.float32) * scale
        s_sel = jnp.where(vi[..., None], s_sel, -jnp.inf).reshape(B, H, MOBA_CHUNK, topk * MOBA_BLOCK)
        s_own = jnp.einsum("bhqd,bhld->bhql", qi, k_own).astype(jnp.float32) * scale
        key_pos = ob * MOBA_BLOCK + jnp.arange(MOBA_BLOCK)
        s_own = jnp.where(key_pos[None, :] <= t[:, None], s_own, -jnp.inf)
        p, _ = _softmax_lse(jnp.concatenate([s_sel, s_own], axis=-1))
        p = p.astype(v.dtype)
        p_sel = p[..., : topk * MOBA_BLOCK].reshape(B, H, MOBA_CHUNK, topk, MOBA_BLOCK)
        p_own = p[..., topk * MOBA_BLOCK:]
        return (jnp.einsum("bhqkl,bhqkld->bhqd", p_sel, v_sel)
                + jnp.einsum("bhql,bhld->bhqd", p_own, v_own))

    o = lax.map(chunk, (jnp.arange(nq), to_chunks(q), to_chunks(sel), to_chunks(valid)))
    return jnp.moveaxis(o, 0, 2).reshape(B, H, S, Dh)


def stick_breaking_attention(q, k, v):
    B, H, S, Dh = q.shape
    nq = S // SB_BLOCK
    key_pos = jnp.arange(S)
    scale = HEAD_DIM ** -0.5
    qc = jnp.moveaxis(q.reshape(B, H, nq, SB_BLOCK, Dh), 2, 0)

    def block(args):
        i, qi = args
        t = i * SB_BLOCK + jnp.arange(SB_BLOCK)
        past = key_pos[None, :] < t[:, None]
        z = jnp.einsum("bhqd,bhsd->bhqs", qi, k).astype(jnp.float32) * scale
        log_keep = jnp.where(past, jax.nn.log_sigmoid(-z), 0.0)
        after = lax.cumsum(log_keep, axis=3, reverse=True) - log_keep
        a = jnp.where(past, jnp.exp(jax.nn.log_sigmoid(z) + after), 0.0)
        return jnp.einsum("bhqs,bhsd->bhqd", a.astype(v.dtype), v)

    o = lax.map(block, (jnp.arange(nq), qc))
    return jnp.moveaxis(o, 0, 2).reshape(B, H, S, Dh)


def hybrid_mixer(h, w_in, w_br_a, w_br_b, w_br_c, w_gate, b_gate, w_out, cos, sin):
    B, S, D = h.shape
    qkv = (h @ w_in).reshape(B, S, 3, N_HEADS, HEAD_DIM).transpose(2, 0, 3, 1, 4)
    q, k, v = qkv[0], qkv[1], qkv[2]
    n_rot = HA + HB
    q_rot = apply_rope(q[:, :n_rot], cos, sin)
    k_rot = apply_rope(k[:, :n_rot], cos, sin)
    oa = dilated_mixture(q_rot[:, :HA], k_rot[:, :HA], v[:, :HA])
    ob = moba_attention(q_rot[:, HA:], k_rot[:, HA:], v[:, HA:n_rot])
    oc = stick_breaking_attention(q[:, n_rot:], k[:, n_rot:], v[:, n_rot:])

    def flat(o):
        return o.transpose(0, 2, 1, 3).reshape(B, S, -1)

    ya = flat(oa) @ w_br_a
    yb = flat(ob) @ w_br_b
    yc = flat(oc) @ w_br_c
    gates = jax.nn.sigmoid(h @ w_gate + b_gate).reshape(B, S, N_BRANCH, D)
    merged = gates[:, :, 0] * ya + gates[:, :, 1] * yb + gates[:, :, 2] * yc
    return merged @ w_out


def swiglu(h, w_gu, w_down):
    gt, up = jnp.split(h @ w_gu, 2, axis=-1)
    return (jax.nn.silu(gt) * up) @ w_down


def setup_inputs(seed: int = 0) -> dict:
    key = jax.random.key(seed)
    ks = jax.random.split(key, 16)

    def nrm(k, shape, fan_in, gain=1.0):
        return jax.random.normal(k, shape, jnp.float32) * (gain * fan_in ** -0.5)

    def gain_vec(k, shape):
        return 1.0 + 0.05 * jax.random.normal(k, shape, jnp.float32)

    D = D_MODEL
    return {
        "x": jax.random.normal(ks[0], (BATCH, SEQ, D), jnp.float32),
        "c": jax.random.normal(ks[1], (BATCH, D), jnp.float32),
        "w_ada": nrm(ks[2], (DEPTH, D, 6 * D), D, 0.5),
        "b_ada": 0.02 * jax.random.normal(ks[3], (DEPTH, 6 * D), jnp.float32),
        "norm1_g": gain_vec(ks[4], (DEPTH, D)),
        "w_in": nrm(ks[5], (DEPTH, D, 3 * MIX_WIDTH), D),
        "w_br_a": nrm(ks[6], (DEPTH, HEADS_PER_DIL * HEAD_DIM, D), HEADS_PER_DIL * HEAD_DIM),
        "w_br_b": nrm(ks[7], (DEPTH, HB * HEAD_DIM, D), HB * HEAD_DIM),
        "w_br_c": nrm(ks[8], (DEPTH, HC * HEAD_DIM, D), HC * HEAD_DIM),
        "w_gate": nrm(ks[9], (DEPTH, D, N_BRANCH * D), D),
        "b_gate": 0.02 * jax.random.normal(ks[10], (DEPTH, N_BRANCH * D), jnp.float32),
        "w_out": nrm(ks[11], (DEPTH, D, D), D),
        "norm2_g": gain_vec(ks[12], (DEPTH, D)),
        "w_gu": nrm(ks[13], (DEPTH, D, 2 * D_FF), D),
        "w_down": nrm(ks[14], (DEPTH, D_FF, D), D_FF),
        "final_g": gain_vec(ks[15], (D,)),
    }


def reference(x, c, w_ada, b_ada, norm1_g, w_in, w_br_a, w_br_b, w_br_c, w_gate, b_gate,
              w_out, norm2_g, w_gu, w_down, final_g):
    S = x.shape[1]
    cos, sin = rope_tables(S)
    c_act = jax.nn.silu(c)
    for l in range(DEPTH):
        mod = c_act @ w_ada[l] + b_ada[l]
        sh1, sc1, g1, sh2, sc2, g2 = [m[:, None, :] for m in jnp.split(mod, 6, axis=-1)]
        h = rms_norm(x, norm1_g[l]) * (1.0 + sc1) + sh1
        x = x + g1 * hybrid_mixer(h, w_in[l], w_br_a[l], w_br_b[l], w_br_c[l],
                                  w_gate[l], b_gate[l], w_out[l], cos, sin)
        h = rms_norm(x, norm2_g[l]) * (1.0 + sc2) + sh2
        x = x + g2 * swiglu(h, w_gu[l], w_down[l])
    return rms_norm(x, final_g)
```

```python
import functools

import jax
import jax.numpy as jnp
from jax import lax
from jax.experimental import pallas as pl
from jax.experimental.pallas import tpu as pltpu

HEAD_DIM = 64
LANES = 128
DIL_GROUPS = ((128, 1), (512, 4), (2048, 16))
HEADS_PER_DIL = 4
HA = len(DIL_GROUPS) * HEADS_PER_DIL
HB = 6
HC = 6
N_HEADS = HA + HB + HC
MIX_WIDTH = N_HEADS * HEAD_DIM
BAND = 128
MOBA_BLOCK = 256
MOBA_TOPK = 3
SB_BLOCK = 256
ROPE_THETA = 10000.0
NORM_EPS = 1e-6
NEG = -1e30
VMEM_LIMIT = 56 * 1024 * 1024

NPA, NPB, NPC = HA // 2, HB // 2, HC // 2
QR0 = 0
KR0 = NPA + NPB
QC0 = 2 * (NPA + NPB)
KC0 = QC0 + NPC
V0 = KC0 + NPC
N_ROPE_COLS = 2 * (NPA + NPB) * LANES


def _f32(x):
    return x.astype(jnp.float32)


def _bf16(x):
    return x.astype(jnp.bfloat16)


def _dot(a, b):
    return jnp.dot(a, b, preferred_element_type=jnp.float32)


def _dot_nt(a, b):
    return lax.dot_general(a, b, (((1,), (1,)), ((), ())), preferred_element_type=jnp.float32)


def _iota(shape, dim):
    return lax.broadcasted_iota(jnp.int32, shape, dim)


def _modulated_norm(x, g, shift, scale):
    ms = jnp.mean(x * x, axis=-1, keepdims=True)
    y = x * lax.rsqrt(ms + NORM_EPS)
    return (y * g) * (1.0 + scale) + shift


def _ada_body(c_ref, w_ref, b_ref, o_ref):
    c = c_ref[...]
    ca = c * jax.nn.sigmoid(c)
    o_ref[...] = jnp.dot(ca, w_ref[...], precision=lax.Precision.HIGHEST,
                         preferred_element_type=jnp.float32) + b_ref[...]


def _ada_call(c, w_ada, b_ada):
    depth, d, n = w_ada.shape
    b = c.shape[0]
    tn = 1536
    return pl.pallas_call(
        _ada_body,
        grid=(depth, n // tn),
        in_specs=[
            pl.BlockSpec((b, d), lambda l, j: (0, 0)),
            pl.BlockSpec((None, d, tn), lambda l, j: (l, 0, j)),
            pl.BlockSpec((None, 1, tn), lambda l, j: (l, 0, j)),
        ],
        out_specs=pl.BlockSpec((None, b, tn), lambda l, j: (l, 0, j)),
        out_shape=jax.ShapeDtypeStruct((depth, b, n), jnp.float32),
        compiler_params=pltpu.CompilerParams(
            dimension_semantics=("arbitrary", "arbitrary"), vmem_limit_bytes=VMEM_LIMIT),
        name="ada_mod",
    )(c, w_ada, b_ada.reshape(depth, 1, n))


def _qkv_body(x_ref, mod_ref, g_ref, w_ref, cos_ref, sin_ref, o_ref, h_ref, *, n_rope_tiles):
    j = pl.program_id(1)

    @pl.when(j == 0)
    def _():
        h = _modulated_norm(x_ref[...], g_ref[...], mod_ref[0:1, :], mod_ref[1:2, :])
        h_ref[...] = _bf16(h)

    res = _dot(h_ref[...], w_ref[...])

    @pl.when(j < n_rope_tiles)
    def _():
        cos = cos_ref[...]
        sin = sin_ref[...]
        first_half = (_iota(cos.shape, 1) % HEAD_DIM) < (HEAD_DIM // 2)
        for g in range(res.shape[1] // LANES):
            blk = res[:, g * LANES:(g + 1) * LANES]
            partner = jnp.where(first_half,
                                pltpu.roll(blk, LANES - HEAD_DIM // 2, 1),
                                pltpu.roll(blk, HEAD_DIM // 2, 1))
            o_ref[:, g * LANES:(g + 1) * LANES] = _bf16(blk * cos + partner * sin)

    @pl.when(j >= n_rope_tiles)
    def _():
        o_ref[...] = _bf16(res)


def _qkv_call(x2, mod, norm_g, w_qkv, cos_t, sin_t, layer, seq):
    t, d = x2.shape
    n = w_qkv.shape[2]
    tm, tn = 1024, 768
    tiles_per_seq = seq // tm
    return pl.pallas_call(
        functools.partial(_qkv_body, n_rope_tiles=N_ROPE_COLS // tn),
        grid=(t // tm, n // tn),
        in_specs=[
            pl.BlockSpec((tm, d), lambda i, j: (i, 0)),
            pl.BlockSpec((None, None, 6, d), lambda i, j: (layer, i // tiles_per_seq, 0, 0)),
            pl.BlockSpec((None, 1, d), lambda i, j: (layer, 0, 0)),
            pl.BlockSpec((None, d, tn), lambda i, j: (layer, 0, j)),
            pl.BlockSpec((tm, LANES), lambda i, j: (i % tiles_per_seq, 0)),
            pl.BlockSpec((tm, LANES), lambda i, j: (i % tiles_per_seq, 0)),
        ],
        out_specs=pl.BlockSpec((tm, tn), lambda i, j: (i, j)),
        out_shape=jax.ShapeDtypeStruct((t, n), jnp.bfloat16),
        scratch_shapes=[pltpu.VMEM((tm, d), jnp.bfloat16)],
        compiler_params=pltpu.CompilerParams(
            dimension_semantics=("arbitrary", "arbitrary"), vmem_limit_bytes=VMEM_LIMIT),
        name="qkv_proj",
    )(x2, mod, norm_g, w_qkv, cos_t, sin_t)


def _band_block(q, k, v, mask, head_lo):
    outs, lses = [], []
    for h in range(2):
        qh = jnp.where(head_lo if h == 0 else ~head_lo, q, jnp.zeros_like(q))
        s = jnp.where(mask, _dot_nt(qh, k), NEG)
        m = jnp.max(s, axis=-1, keepdims=True)
        e = jnp.exp(s - m)
        den = jnp.sum(e, axis=-1, keepdims=True)
        outs.append(_dot(_bf16(e), v) / den)
        lses.append(m + jnp.log(den))
    o = jnp.where(head_lo, outs[0], outs[1])
    lse = jnp.where(head_lo, lses[0], lses[1])
    return o, lse


def _dil_body(q1, k1, v1, q2, k2, v2, q3, k3, v3, o_ref, f_ref, og_ref, lg_ref, *, seq):
    head_lo = _iota((BAND, LANES), 1) < HEAD_DIM
    row = _iota((BAND, 2 * BAND), 0)
    col = _iota((BAND, 2 * BAND), 1)
    mask_win = (col >= row) & (col <= row + BAND)
    mask_own = _iota((BAND, BAND), 1) <= _iota((BAND, BAND), 0)

    srcs = ((q1, k1, v1), (q2, k2, v2), (q3, k3, v3))
    for g, (_, dil) in enumerate(DIL_GROUPS):
        for a in range(3):
            f_ref[a] = _f32(srcs[g][a][...])
        sub_len = seq // dil
        n_blk = sub_len // BAND

        def load(a, start, size, dil=dil):
            return _bf16(f_ref[a, pl.ds(start, size, stride=dil), :])

        def store(start, o, lse, g=g, dil=dil):
            og_ref[g, pl.ds(start, BAND, stride=dil), :] = o
            lg_ref[g, pl.ds(start, BAND, stride=dil), :] = lse

        def residue(r, carry, load=load, store=store, dil=dil, n_blk=n_blk):
            o, lse = _band_block(load(0, r, BAND), load(1, r, BAND), load(2, r, BAND),
                                 mask_own, head_lo)
            store(r, o, lse)

            def block(qb, c):
                q0 = r + qb * (BAND * dil)
                k0 = q0 - BAND * dil
                o, lse = _band_block(load(0, q0, BAND), load(1, k0, 2 * BAND),
                                     load(2, k0, 2 * BAND), mask_win, head_lo)
                store(q0, o, lse)
                return c

            lax.fori_loop(1, n_blk, block, 0)
            return carry

        lax.fori_loop(0, dil, residue, 0)

    l1, l2, l3 = lg_ref[0], lg_ref[1], lg_ref[2]
    mx = jnp.maximum(jnp.maximum(l1, l2), l3)
    w1, w2, w3 = jnp.exp(l1 - mx), jnp.exp(l2 - mx), jnp.exp(l3 - mx)
    o = (w1 * og_ref[0] + w2 * og_ref[1] + w3 * og_ref[2]) / (w1 + w2 + w3)
    o_ref[...] = _bf16(o)


def _dil_call(qkv3):
    b, seq, _ = qkv3.shape
    n_slot_pairs = HEADS_PER_DIL // 2

    def spec(base, g):
        return pl.BlockSpec((None, seq, LANES), lambda bi, p: (bi, 0, base + g * n_slot_pairs + p))

    in_specs = []
    for g in range(len(DIL_GROUPS)):
        in_specs += [spec(QR0, g), spec(KR0, g), spec(V0, g)]
    return pl.pallas_call(
        functools.partial(_dil_body, seq=seq),
        grid=(b, n_slot_pairs),
        in_specs=in_specs,
        out_specs=pl.BlockSpec((None, seq, LANES), lambda bi, p: (bi, 0, p)),
        out_shape=jax.ShapeDtypeStruct((b, seq, n_slot_pairs * LANES), jnp.bfloat16),
        scratch_shapes=[
            pltpu.VMEM((3, seq, LANES), jnp.float32),
            pltpu.VMEM((len(DIL_GROUPS), seq, LANES), jnp.float32),
            pltpu.VMEM((len(DIL_GROUPS), seq, LANES), jnp.float32),
        ],
        compiler_params=pltpu.CompilerParams(
            dimension_semantics=("arbitrary", "arbitrary"), vmem_limit_bytes=VMEM_LIMIT),
        name="dilated_attn",
    )(*([qkv3] * 9))


def _moba_body(q_ref, k_ref, v_ref, o_ref, kmean_ref, acc_ref, m_ref, l_ref, *, n_blk):
    i = pl.program_id(2)
    blk = MOBA_BLOCK

    @pl.when(i == 0)
    def _():
        kmean_ref[...] = jnp.zeros_like(kmean_ref)
        for j in range(n_blk):
            kj = _f32(k_ref[j * blk:(j + 1) * blk, :])
            kmean_ref[j:j + 1, :] = jnp.mean(kj, axis=0, keepdims=True)

    q = q_ref[...]
    head_lo = _iota(q.shape, 1) < HEAD_DIM
    lane = _iota((blk, LANES), 1)
    km = kmean_ref[...]
    km_hi = _bf16(km)
    km_mid = _bf16(km - _f32(km_hi))
    km_lo = _bf16(km - _f32(km_hi) - _f32(km_mid))
    row = _iota((blk, blk), 0)
    col = _iota((blk, blk), 1)
    causal = col <= row
    k_own = k_ref[pl.ds(pl.multiple_of(i * blk, blk), blk), :]
    v_own = v_ref[pl.ds(pl.multiple_of(i * blk, blk), blk), :]

    for h in range(2):
        qh = jnp.where(head_lo if h == 0 else ~head_lo, q, jnp.zeros_like(q))
        gate = _dot_nt(qh, km_hi) + _dot_nt(qh, km_mid) + _dot_nt(qh, km_lo)
        past = lane < i
        selmat = jnp.zeros((blk, LANES), jnp.float32)
        for j in range(n_blk - 1):
            gj = gate[:, j:j + 1]
            beats = ((gate > gj) | ((gate == gj) & (lane < j))) & past
            rank = jnp.sum(jnp.where(beats, 1.0, 0.0), axis=-1, keepdims=True)
            selmat = jnp.where((lane == j) & (rank < MOBA_TOPK), 1.0, selmat)

        s = jnp.where(causal, _dot_nt(qh, k_own), NEG)
        m = jnp.max(s, axis=-1, keepdims=True)
        p = jnp.exp(s - m)
        m_ref[h] = m
        l_ref[h] = jnp.sum(p, axis=-1, keepdims=True)
        acc_ref[h] = _dot(_bf16(p), v_own)

        def past_block(j, c, qh=qh, selmat=selmat, h=h):
            sel = jnp.sum(jnp.where(lane == j, selmat, 0.0), axis=-1, keepdims=True) > 0.5
            kj = k_ref[pl.ds(pl.multiple_of(j * blk, blk), blk), :]
            vj = v_ref[pl.ds(pl.multiple_of(j * blk, blk), blk), :]
            s = jnp.where(sel, _dot_nt(qh, kj), NEG)
            m_old = m_ref[h]
            m_new = jnp.maximum(m_old, jnp.max(s, axis=-1, keepdims=True))
            alpha = jnp.exp(m_old - m_new)
            p = jnp.exp(s - m_new)
            m_ref[h] = m_new
            l_ref[h] = alpha * l_ref[h] + jnp.sum(p, axis=-1, keepdims=True)
            acc_ref[h] = alpha * acc_ref[h] + _dot(_bf16(p), vj)
            return c

        lax.fori_loop(0, i, past_block, 0)

    o = jnp.where(head_lo, acc_ref[0] / l_ref[0], acc_ref[1] / l_ref[1])
    o_ref[...] = _bf16(o)


def _moba_call(qkv3):
    b, seq, _ = qkv3.shape
    n_blk = seq // MOBA_BLOCK
    blk = MOBA_BLOCK
    return pl.pallas_call(
        functools.partial(_moba_body, n_blk=n_blk),
        grid=(b, NPB, n_blk),
        in_specs=[
            pl.BlockSpec((None, blk, LANES), lambda bi, p, i: (bi, i, QR0 + NPA + p)),
            pl.BlockSpec((None, seq, LANES), lambda bi, p, i: (bi, 0, KR0 + NPA + p)),
            pl.BlockSpec((None, seq, LANES), lambda bi, p, i: (bi, 0, V0 + NPA + p)),
        ],
        out_specs=pl.BlockSpec((None, blk, LANES), lambda bi, p, i: (bi, i, p)),
        out_shape=jax.ShapeDtypeStruct((b, seq, NPB * LANES), jnp.bfloat16),
        scratch_shapes=[
            pltpu.VMEM((LANES, LANES), jnp.float32),
            pltpu.VMEM((2, blk, LANES), jnp.float32),
            pltpu.VMEM((2, blk, 1), jnp.float32),
            pltpu.VMEM((2, blk, 1), jnp.float32),
        ],
        compiler_params=pltpu.CompilerParams(
            dimension_semantics=("arbitrary", "arbitrary", "arbitrary"), vmem_limit_bytes=VMEM_LIMIT),
        name="moba_attn",
    )(qkv3, qkv3, qkv3)


def _sb_block(qh, k, v, upper, r_prev, past):
    z = _dot_nt(qh, k)
    l1p = jnp.log1p(jnp.exp(-jnp.abs(z)))
    ls = jnp.minimum(z, 0.0) - l1p
    lk = ls - z
    if past is not None:
        lk = jnp.where(past, lk, 0.0)
    hi = _bf16(lk)
    lo = _bf16(lk - _f32(hi))
    after = _dot(hi, upper) + _dot(lo, upper)
    a = jnp.exp(ls + after + r_prev)
    if past is not None:
        a = jnp.where(past, a, 0.0)
    r_new = r_prev + after[:, 0:1] + lk[:, 0:1]
    return _dot(_bf16(a), v), r_new


def _sb_body(q_ref, k_ref, v_ref, o_ref, acc_ref, r_ref):
    i = pl.program_id(2)
    blk = SB_BLOCK
    q = q_ref[...]
    head_lo = _iota(q.shape, 1) < HEAD_DIM
    row = _iota((blk, blk), 0)
    col = _iota((blk, blk), 1)
    past = col < row
    upper = jnp.where(row > col, 1.0, 0.0).astype(jnp.bfloat16)
    k_own = k_ref[pl.ds(pl.multiple_of(i * blk, blk), blk), :]
    v_own = v_ref[pl.ds(pl.multiple_of(i * blk, blk), blk), :]
    qhs = [jnp.where(head_lo if h == 0 else ~head_lo, q, jnp.zeros_like(q)) for h in range(2)]

    for h in range(2):
        acc, r = _sb_block(qhs[h], k_own, v_own, upper, jnp.zeros((blk, 1), jnp.float32), past)
        acc_ref[h] = acc
        r_ref[h] = r

    def earlier_block(t, c):
        j = i - 1 - t
        kj = k_ref[pl.ds(pl.multiple_of(j * blk, blk), blk), :]
        vj = v_ref[pl.ds(pl.multiple_of(j * blk, blk), blk), :]
        for h in range(2):
            acc, r = _sb_block(qhs[h], kj, vj, upper, r_ref[h], None)
            acc_ref[h] = acc_ref[h] + acc
            r_ref[h] = r
        return c

    lax.fori_loop(0, i, earlier_block, 0)
    o_ref[...] = _bf16(jnp.where(head_lo, acc_ref[0], acc_ref[1]))


def _sb_call(qkv3):
    b, seq, _ = qkv3.shape
    blk = SB_BLOCK
    return pl.pallas_call(
        _sb_body,
        grid=(b, NPC, seq // blk),
        in_specs=[
            pl.BlockSpec((None, blk, LANES), lambda bi, p, i: (bi, i, QC0 + p)),
            pl.BlockSpec((None, seq, LANES), lambda bi, p, i: (bi, 0, KC0 + p)),
            pl.BlockSpec((None, seq, LANES), lambda bi, p, i: (bi, 0, V0 + NPA + NPB + p)),
        ],
        out_specs=pl.BlockSpec((None, blk, LANES), lambda bi, p, i: (bi, i, p)),
        out_shape=jax.ShapeDtypeStruct((b, seq, NPC * LANES), jnp.bfloat16),
        scratch_shapes=[
            pltpu.VMEM((2, blk, LANES), jnp.float32),
            pltpu.VMEM((2, blk, 1), jnp.float32),
        ],
        compiler_params=pltpu.CompilerParams(
            dimension_semantics=("arbitrary", "arbitrary", "arbitrary"), vmem_limit_bytes=VMEM_LIMIT),
        name="stickbreak_attn",
    )(qkv3, qkv3, qkv3)


def _merge_body(x_ref, mod_ref, g_ref, oa_ref, ob_ref, oc_ref, wa_ref, wb_ref, wc_ref,
                wg_ref, bg_ref, wo_ref, out_ref):
    x = x_ref[...]
    d = x.shape[1]
    h = _bf16(_modulated_norm(x, g_ref[...], mod_ref[0:1, :], mod_ref[1:2, :]))
    merged = None
    for br, (o_ref, w_ref) in enumerate(((oa_ref, wa_ref), (ob_ref, wb_ref), (oc_ref, wc_ref))):
        gate = jax.nn.sigmoid(_dot(h, wg_ref[:, br * d:(br + 1) * d]) + bg_ref[:, br * d:(br + 1) * d])
        term = gate * _dot(o_ref[...], w_ref[...])
        merged = term if merged is None else merged + term
    out_ref[...] = x + mod_ref[2:3, :] * _dot(_bf16(merged), wo_ref[...])


def _merge_call(x2, mod, norm_g, oa, ob, oc, w_br_a, w_br_b, w_br_c, w_gate, b_gate, w_out, layer, seq):
    t, d = x2.shape
    tm = 512
    tiles_per_seq = seq // tm

    def whole(arr):
        return pl.BlockSpec((None,) + arr.shape[1:], lambda i: (layer,) + (0,) * (arr.ndim - 1))

    def rows(arr):
        return pl.BlockSpec((tm, arr.shape[1]), lambda i: (i, 0))

    return pl.pallas_call(
        _merge_body,
        grid=(t // tm,),
        in_specs=[
            rows(x2),
            pl.BlockSpec((None, None, 6, d), lambda i: (layer, i // tiles_per_seq, 0, 0)),
            whole(norm_g), rows(oa), rows(ob), rows(oc),
            whole(w_br_a), whole(w_br_b), whole(w_br_c), whole(w_gate), whole(b_gate), whole(w_out),
        ],
        out_specs=rows(x2),
        out_shape=jax.ShapeDtypeStruct((t, d), jnp.float32),
        compiler_params=pltpu.CompilerParams(
            dimension_semantics=("arbitrary",), vmem_limit_bytes=VMEM_LIMIT),
        name="gated_merge",
    )(x2, mod, norm_g, oa, ob, oc, w_br_a, w_br_b, w_br_c, w_gate, b_gate, w_out)


def _ffn_body(x_ref, mod_ref, g_ref, wg_ref, wu_ref, wd_ref, out_ref, h_ref, acc_ref):
    j = pl.program_id(1)

    @pl.when(j == 0)
    def _():
        h = _modulated_norm(x_ref[...], g_ref[...], mod_ref[3:4, :], mod_ref[4:5, :])
        h_ref[...] = _bf16(h)
        acc_ref[...] = jnp.zeros_like(acc_ref)

    h = h_ref[...]
    gt = _dot(h, wg_ref[...])
    up = _dot(h, wu_ref[...])
    act = (gt * jax.nn.sigmoid(gt)) * up
    acc_ref[...] += _dot(_bf16(act), wd_ref[...])

    @pl.when(j == pl.num_programs(1) - 1)
    def _():
        out_ref[...] = x_ref[...] + mod_ref[5:6, :] * acc_ref[...]


def _ffn_call(x2, mod, norm_g, w_gu, w_down, layer, seq):
    t, d = x2.shape
    d_ff = w_down.shape[1]
    tm, tf = 1024, 256
    n_ff = d_ff // tf
    tiles_per_seq = seq // tm
    return pl.pallas_call(
        _ffn_body,
        grid=(t // tm, n_ff),
        in_specs=[
            pl.BlockSpec((tm, d), lambda i, j: (i, 0)),
            pl.BlockSpec((None, None, 6, d), lambda i, j: (layer, i // tiles_per_seq, 0, 0)),
            pl.BlockSpec((None, 1, d), lambda i, j: (layer, 0, 0)),
            pl.BlockSpec((None, d, tf), lambda i, j: (layer, 0, j)),
            pl.BlockSpec((None, d, tf), lambda i, j: (layer, 0, n_ff + j)),
            pl.BlockSpec((None, tf, d), lambda i, j: (layer, j, 0)),
        ],
        out_specs=pl.BlockSpec((tm, d), lambda i, j: (i, 0)),
        out_shape=jax.ShapeDtypeStruct((t, d), jnp.float32),
        scratch_shapes=[pltpu.VMEM((tm, d), jnp.bfloat16), pltpu.VMEM((tm, d), jnp.float32)],
        compiler_params=pltpu.CompilerParams(
            dimension_semantics=("arbitrary", "arbitrary"), vmem_limit_bytes=VMEM_LIMIT),
        name="swiglu_ffn",
    )(x2, mod, norm_g, w_gu, w_gu, w_down)


def _final_body(x_ref, g_ref, o_ref):
    x = x_ref[...]
    ms = jnp.mean(x * x, axis=-1, keepdims=True)
    o_ref[...] = (x * lax.rsqrt(ms + NORM_EPS)) * g_ref[...]


def _final_call(x2, g):
    t, d = x2.shape
    tm = 1024
    return pl.pallas_call(
        _final_body,
        grid=(t // tm,),
        in_specs=[pl.BlockSpec((tm, d), lambda i: (i, 0)), pl.BlockSpec((1, d), lambda i: (0, 0))],
        out_specs=pl.BlockSpec((tm, d), lambda i: (i, 0)),
        out_shape=jax.ShapeDtypeStruct((t, d), jnp.float32),
        compiler_params=pltpu.CompilerParams(
            dimension_semantics=("arbitrary",), vmem_limit_bytes=VMEM_LIMIT),
        name="final_norm",
    )(x2, g.reshape(1, d))


def _rope_tables(seq):
    pos = jnp.arange(seq, dtype=jnp.float32)
    inv = ROPE_THETA ** (-jnp.arange(0, HEAD_DIM, 2, dtype=jnp.float32) / HEAD_DIM)
    ang = pos[:, None] * inv[None, :]
    cos, sin = jnp.cos(ang), jnp.sin(ang)
    cos_t = jnp.tile(cos, (1, 2 * LANES // HEAD_DIM))
    sin_t = jnp.tile(jnp.concatenate([-sin, sin], axis=-1), (1, LANES // HEAD_DIM))
    return cos_t, sin_t


def _arrange_w_in(w_in):
    n_rot = (HA + HB) * HEAD_DIM
    q = w_in[:, :, :MIX_WIDTH] * (HEAD_DIM ** -0.5)
    k = w_in[:, :, MIX_WIDTH:2 * MIX_WIDTH]
    v = w_in[:, :, 2 * MIX_WIDTH:]
    return _bf16(jnp.concatenate([q[:, :, :n_rot], k[:, :, :n_rot], q[:, :, n_rot:], k[:, :, n_rot:], v],
                                 axis=-1))


def kernel(x, c, w_ada, b_ada, norm1_g, w_in, w_br_a, w_br_b, w_br_c, w_gate, b_gate, w_out,
           norm2_g, w_gu, w_down, final_g):
    b, seq, d = x.shape
    depth = w_in.shape[0]
    cos_t, sin_t = _rope_tables(seq)
    w_qkv = _arrange_w_in(w_in)
    w_br_a, w_br_b, w_br_c = _bf16(w_br_a), _bf16(w_br_b), _bf16(w_br_c)
    w_gate, w_out, w_gu, w_down = _bf16(w_gate), _bf16(w_out), _bf16(w_gu), _bf16(w_down)
    b_gate3 = b_gate.reshape(depth, 1, -1)
    g1 = norm1_g.reshape(depth, 1, d)
    g2 = norm2_g.reshape(depth, 1, d)

    mod = _ada_call(c, w_ada, b_ada).reshape(depth, b, 6, d)
    x2 = x.reshape(b * seq, d)
    for layer in range(depth):
        qkv = _qkv_call(x2, mod, g1, w_qkv, cos_t, sin_t, layer, seq)
        qkv3 = qkv.reshape(b, seq, -1)
        oa = _dil_call(qkv3).reshape(b * seq, -1)
        ob = _moba_call(qkv3).reshape(b * seq, -1)
        oc = _sb_call(qkv3).reshape(b * seq, -1)
        x2 = _merge_call(x2, mod, g1, oa, ob, oc, w_br_a, w_br_b, w_br_c, w_gate, b_gate3, w_out,
                         layer, seq)
        x2 = _ffn_call(x2, mod, g2, w_gu, w_down, layer, seq)
    return _final_call(x2, final_g).reshape(b, seq, d)
```

```python
import functools

import jax
import jax.numpy as jnp
from jax import lax
from jax.experimental import pallas as pl
from jax.experimental.pallas import tpu as pltpu

HEAD_DIM = 64
LANES = 128
DIL_GROUPS = ((128, 1), (512, 4), (2048, 16))
HEADS_PER_DIL = 4
HA = len(DIL_GROUPS) * HEADS_PER_DIL
HB = 6
HC = 6
N_HEADS = HA + HB + HC
MIX_WIDTH = N_HEADS * HEAD_DIM
BAND = 128
BLK = 256
MOBA_TOPK = 3
KMEAN_ROWS = 16
ROPE_THETA = 10000.0
NORM_EPS = 1e-6
NEG = -1e30
VMEM_LIMIT = 56 * 1024 * 1024

NPA, NPB, NPC = HA // 2, HB // 2, HC // 2
QR0 = 0
KR0 = NPA + NPB
QC0 = 2 * (NPA + NPB)
KC0 = QC0 + NPC
V0 = KC0 + NPC
N_ROPE_COLS = 2 * (NPA + NPB) * LANES
QKV_TN = 768


def _f32(x):
    return x.astype(jnp.float32)


def _bf16(x):
    return x.astype(jnp.bfloat16)


def _dot(a, b):
    return jnp.dot(a, b, preferred_element_type=jnp.float32)


def _dot_nt(a, b):
    return lax.dot_general(a, b, (((1,), (1,)), ((), ())), preferred_element_type=jnp.float32)


def _iota(shape, dim):
    return lax.broadcasted_iota(jnp.int32, shape, dim)


def _modulated_norm(x, g, shift, scale):
    ms = jnp.mean(x * x, axis=-1, keepdims=True)
    y = x * lax.rsqrt(ms + NORM_EPS)
    return (y * g) * (1.0 + scale) + shift


def _split_heads(q):
    head_lo = _iota(q.shape, 1) < HEAD_DIM
    zero = jnp.zeros_like(q)
    return jnp.where(head_lo, q, zero), jnp.where(head_lo, zero, q)


def _ada_body(c_ref, w_ref, b_ref, o_ref):
    c = c_ref[...]
    ca = c * jax.nn.sigmoid(c)
    o_ref[...] = jnp.dot(ca, w_ref[...], precision=lax.Precision.HIGHEST,
                         preferred_element_type=jnp.float32) + b_ref[...]


def _ada_call(c, w_ada, b_ada):
    depth, d, n = w_ada.shape
    b = c.shape[0]
    tn = 1536
    return pl.pallas_call(
        _ada_body,
        grid=(depth, n // tn),
        in_specs=[
            pl.BlockSpec((b, d), lambda l, j: (0, 0)),
            pl.BlockSpec((None, d, tn), lambda l, j: (l, 0, j)),
            pl.BlockSpec((None, 1, tn), lambda l, j: (l, 0, j)),
        ],
        out_specs=pl.BlockSpec((None, b, tn), lambda l, j: (l, 0, j)),
        out_shape=jax.ShapeDtypeStruct((depth, b, n), jnp.float32),
        compiler_params=pltpu.CompilerParams(
            dimension_semantics=("arbitrary", "arbitrary"), vmem_limit_bytes=VMEM_LIMIT),
        name="ada_mod",
    )(c, w_ada, b_ada.reshape(depth, 1, n))


def _qkv_body(x_ref, mod_ref, g_ref, w_ref, cos_ref, sin_ref, o_ref, h_ref, *, n_rope_tiles):
    j = pl.program_id(1)

    @pl.when(j == 0)
    def _():
        h = _modulated_norm(x_ref[...], g_ref[...], mod_ref[0:1, :], mod_ref[1:2, :])
        h_ref[...] = _bf16(h)

    res = _dot(h_ref[...], w_ref[...])

    @pl.when(j < n_rope_tiles)
    def _():
        cos = cos_ref[...]
        sin = sin_ref[...]
        first_half = (_iota(cos.shape, 1) % HEAD_DIM) < (HEAD_DIM // 2)
        for g in range(res.shape[1] // LANES):
            blk = res[:, g * LANES:(g + 1) * LANES]
            partner = jnp.where(first_half,
                                pltpu.roll(blk, LANES - HEAD_DIM // 2, 1),
                                pltpu.roll(blk, HEAD_DIM // 2, 1))
            o_ref[:, g * LANES:(g + 1) * LANES] = _bf16(blk * cos + partner * sin)

    @pl.when(j >= n_rope_tiles)
    def _():
        o_ref[...] = _bf16(res)


def _qkv_call(x2, mod, norm_g, w_qkv, cos_t, sin_t, layer, seq):
    t, d = x2.shape
    n = w_qkv.shape[2]
    tm, tn = 1024, QKV_TN
    tiles_per_seq = seq // tm
    return pl.pallas_call(
        functools.partial(_qkv_body, n_rope_tiles=N_ROPE_COLS // tn),
        grid=(t // tm, n // tn),
        in_specs=[
            pl.BlockSpec((tm, d), lambda i, j: (i, 0)),
            pl.BlockSpec((None, None, 6, d), lambda i, j: (layer, i // tiles_per_seq, 0, 0)),
            pl.BlockSpec((None, 1, d), lambda i, j: (layer, 0, 0)),
            pl.BlockSpec((None, d, tn), lambda i, j: (layer, 0, j)),
            pl.BlockSpec((tm, LANES), lambda i, j: (i % tiles_per_seq, 0)),
            pl.BlockSpec((tm, LANES), lambda i, j: (i % tiles_per_seq, 0)),
        ],
        out_specs=pl.BlockSpec((tm, tn), lambda i, j: (i, j)),
        out_shape=jax.ShapeDtypeStruct((t, n), jnp.bfloat16),
        scratch_shapes=[pltpu.VMEM((tm, d), jnp.bfloat16)],
        compiler_params=pltpu.CompilerParams(
            dimension_semantics=("arbitrary", "arbitrary"), vmem_limit_bytes=VMEM_LIMIT),
        name="qkv_proj",
    )(x2, mod, norm_g, w_qkv, cos_t, sin_t)


def _band_block(q, k, v, mask, head_lo):
    outs, lses = [], []
    for qh in _split_heads(q):
        s = jnp.where(mask, _dot_nt(qh, k), NEG)
        m = jnp.max(s, axis=-1, keepdims=True)
        e = jnp.exp(s - m)
        den = jnp.sum(e, axis=-1, keepdims=True)
        outs.append(_dot(_bf16(e), v) / den)
        lses.append(m + jnp.log(den))
    o = jnp.where(head_lo, outs[0], outs[1])
    lse = jnp.where(head_lo, lses[0], lses[1])
    return o, lse


def _dil_body(q1, k1, v1, q2, k2, v2, q3, k3, v3, o_ref, f_ref, og_ref, lg_ref, *, seq):
    head_lo = _iota((BAND, LANES), 1) < HEAD_DIM
    row = _iota((BAND, 2 * BAND), 0)
    col = _iota((BAND, 2 * BAND), 1)
    mask_win = (col >= row) & (col <= row + BAND)
    mask_own = _iota((BAND, BAND), 1) <= _iota((BAND, BAND), 0)

    srcs = ((q1, k1, v1), (q2, k2, v2), (q3, k3, v3))
    for g, (_, dil) in enumerate(DIL_GROUPS):
        for a in range(3):
            f_ref[a] = _f32(srcs[g][a][...])
        sub_len = seq // dil
        for r in range(dil):
            qs, ks, vs = (_bf16(f_ref[a, pl.ds(r, sub_len, stride=dil), :]) for a in range(3))
            o_blocks, lse_blocks = [], []
            for qb in range(sub_len // BAND):
                q = qs[qb * BAND:(qb + 1) * BAND]
                if qb == 0:
                    o, lse = _band_block(q, ks[:BAND], vs[:BAND], mask_own, head_lo)
                else:
                    o, lse = _band_block(q, ks[(qb - 1) * BAND:(qb + 1) * BAND],
                                         vs[(qb - 1) * BAND:(qb + 1) * BAND], mask_win, head_lo)
                o_blocks.append(o)
                lse_blocks.append(lse)
            og_ref[g, pl.ds(r, sub_len, stride=dil), :] = jnp.concatenate(o_blocks, axis=0)
            lg_ref[g, pl.ds(r, sub_len, stride=dil), :] = jnp.concatenate(lse_blocks, axis=0)

    l1, l2, l3 = lg_ref[0], lg_ref[1], lg_ref[2]
    mx = jnp.maximum(jnp.maximum(l1, l2), l3)
    w1, w2, w3 = jnp.exp(l1 - mx), jnp.exp(l2 - mx), jnp.exp(l3 - mx)
    o = (w1 * og_ref[0] + w2 * og_ref[1] + w3 * og_ref[2]) / (w1 + w2 + w3)
    o_ref[...] = _bf16(o)


def _dil_call(qkv3):
    b, seq, _ = qkv3.shape
    n_slot_pairs = HEADS_PER_DIL // 2

    def spec(base, g):
        return pl.BlockSpec((None, seq, LANES), lambda bi, p: (bi, 0, base + g * n_slot_pairs + p))

    in_specs = []
    for g in range(len(DIL_GROUPS)):
        in_specs += [spec(QR0, g), spec(KR0, g), spec(V0, g)]
    return pl.pallas_call(
        functools.partial(_dil_body, seq=seq),
        grid=(b, n_slot_pairs),
        in_specs=in_specs,
        out_specs=pl.BlockSpec((None, seq, LANES), lambda bi, p: (bi, 0, p)),
        out_shape=jax.ShapeDtypeStruct((b, seq, n_slot_pairs * LANES), jnp.bfloat16),
        scratch_shapes=[
            pltpu.VMEM((3, seq, LANES), jnp.float32),
            pltpu.VMEM((len(DIL_GROUPS), seq, LANES), jnp.float32),
            pltpu.VMEM((len(DIL_GROUPS), seq, LANES), jnp.float32),
        ],
        compiler_params=pltpu.CompilerParams(
            dimension_semantics=("arbitrary", "arbitrary"), vmem_limit_bytes=VMEM_LIMIT),
        name="dilated_attn",
    )(*([qkv3] * 9))


def _moba_select(qh, kmean, n_past):
    km_hi = _bf16(kmean)
    km_mid = _bf16(kmean - _f32(km_hi))
    km_lo = _bf16(kmean - _f32(km_hi) - _f32(km_mid))
    gate = _dot_nt(km_hi, qh) + _dot_nt(km_mid, qh) + _dot_nt(km_lo, qh)
    blk_id = _iota(gate.shape, 0)
    past = blk_id < n_past
    sel = jnp.zeros(gate.shape, jnp.float32)
    for j in range(n_past):
        gj = gate[j:j + 1, :]
        beats = ((gate > gj) | ((gate == gj) & (blk_id < j))) & past
        rank = jnp.sum(jnp.where(beats, 1.0, 0.0), axis=0, keepdims=True)
        sel = jnp.where((blk_id == j) & (rank < MOBA_TOPK), 1.0, sel)
    pad = jnp.zeros((LANES - sel.shape[0], sel.shape[1]), jnp.float32)
    return jnp.concatenate([sel, pad], axis=0).T


def _moba_block(c, q_ref, k_ref, v_ref, o_ref, kmean_ref):
    n = (c + 1) * BLK
    qhs = _split_heads(q_ref[...])
    head_lo = _iota((BLK, LANES), 1) < HEAD_DIM
    causal = _iota((BLK, BLK), 1) <= _iota((BLK, BLK), 0)
    k = k_ref[0:n, :]
    v = v_ref[0:n, :]
    v_lo = _iota(v.shape, 1) < HEAD_DIM
    one = jnp.ones_like(v)
    vhs = (jnp.where(v_lo, v, one), jnp.where(v_lo, one, v))
    outs = []
    for h in range(2):
        s = _dot_nt(qhs[h], k)
        parts = []
        if c > 0:
            sel = _moba_select(qhs[h], kmean_ref[...], c)
            for j in range(c):
                parts.append(jnp.where(sel[:, j:j + 1] > 0.5, s[:, j * BLK:(j + 1) * BLK], NEG))
        parts.append(jnp.where(causal, s[:, c * BLK:], NEG))
        mx = parts[0]
        for part in parts[1:]:
            mx = jnp.maximum(mx, part)
        m = jnp.max(mx, axis=-1, keepdims=True)
        p = jnp.concatenate([_bf16(jnp.exp(part - m)) for part in parts], axis=1)
        acc = _dot(p, vhs[h])
        outs.append(acc / pltpu.roll(acc, HEAD_DIM, 1))
    o_ref[...] = _bf16(jnp.where(head_lo, outs[0], outs[1]))


def _moba_body(q_ref, k_ref, v_ref, o_ref, kmean_ref, *, n_blk):
    i = pl.program_id(2)

    @pl.when(i == 0)
    def _():
        kmean_ref[...] = jnp.zeros_like(kmean_ref)
        for j in range(n_blk):
            kj = _f32(k_ref[j * BLK:(j + 1) * BLK, :])
            kmean_ref[j:j + 1, :] = jnp.mean(kj, axis=0, keepdims=True)

    for c in range(n_blk):
        pl.when(i == c)(functools.partial(_moba_block, c, q_ref, k_ref, v_ref, o_ref, kmean_ref))


def _moba_call(qkv3):
    b, seq, _ = qkv3.shape
    n_blk = seq // BLK
    return pl.pallas_call(
        functools.partial(_moba_body, n_blk=n_blk),
        grid=(b, NPB, n_blk),
        in_specs=[
            pl.BlockSpec((None, BLK, LANES), lambda bi, p, i: (bi, i, QR0 + NPA + p)),
            pl.BlockSpec((None, seq, LANES), lambda bi, p, i: (bi, 0, KR0 + NPA + p)),
            pl.BlockSpec((None, seq, LANES), lambda bi, p, i: (bi, 0, V0 + NPA + p)),
        ],
        out_specs=pl.BlockSpec((None, BLK, LANES), lambda bi, p, i: (bi, i, p)),
        out_shape=jax.ShapeDtypeStruct((b, seq, NPB * LANES), jnp.bfloat16),
        scratch_shapes=[pltpu.VMEM((KMEAN_ROWS, LANES), jnp.float32)],
        compiler_params=pltpu.CompilerParams(
            dimension_semantics=("arbitrary", "arbitrary", "arbitrary"), vmem_limit_bytes=VMEM_LIMIT),
        name="moba_attn",
    )(qkv3, qkv3, qkv3)


def _sb_block(c, q_ref, k_ref, v_ref, o_ref):
    n = (c + 1) * BLK
    qhs = _split_heads(q_ref[...])
    head_lo = _iota((BLK, LANES), 1) < HEAD_DIM
    row = _iota((BLK, BLK), 0)
    col = _iota((BLK, BLK), 1)
    past = col < row
    later = jnp.where(row > col, 1.0, 0.0).astype(jnp.bfloat16)
    later2 = jnp.concatenate([later, later], axis=0)
    k = k_ref[0:n, :]
    v = v_ref[0:n, :]
    accs = []
    for h in range(2):
        z_all = _dot_nt(qhs[h], k)
        ls_parts, after_parts, sums = [], [], []
        for j in range(c + 1):
            z = z_all[:, j * BLK:(j + 1) * BLK]
            ls = jnp.minimum(z, 0.0) - jnp.log(1.0 + jnp.exp(-jnp.abs(z)))
            lk = ls - z
            if j == c:
                lk = jnp.where(past, lk, 0.0)
            hi = _bf16(lk)
            lo = _bf16(lk - _f32(hi))
            after = _dot(jnp.concatenate([hi, lo], axis=1), later2)
            ls_parts.append(ls)
            after_parts.append(after)
            sums.append(after[:, 0:1] + lk[:, 0:1])
        a_parts = [None] * (c + 1)
        beyond = None
        for j in range(c, -1, -1):
            t = ls_parts[j] + after_parts[j]
            if beyond is not None:
                t = t + beyond
            a = jnp.exp(t)
            if j == c:
                a = jnp.where(past, a, 0.0)
            a_parts[j] = _bf16(a)
            beyond = sums[j] if beyond is None else beyond + sums[j]
        accs.append(_dot(jnp.concatenate(a_parts, axis=1), v))
    o_ref[...] = _bf16(jnp.where(head_lo, accs[0], accs[1]))


def _sb_body(q_ref, k_ref, v_ref, o_ref, *, n_blk):
    i = pl.program_id(2)
    for c in range(n_blk):
        pl.when(i == c)(functools.partial(_sb_block, c, q_ref, k_ref, v_ref, o_ref))


def _sb_call(qkv3):
    b, seq, _ = qkv3.shape
    n_blk = seq // BLK
    return pl.pallas_call(
        functools.partial(_sb_body, n_blk=n_blk),
        grid=(b, NPC, n_blk),
        in_specs=[
            pl.BlockSpec((None, BLK, LANES), lambda bi, p, i: (bi, i, QC0 + p)),
            pl.BlockSpec((None, seq, LANES), lambda bi, p, i: (bi, 0, KC0 + p)),
            pl.BlockSpec((None, seq, LANES), lambda bi, p, i: (bi, 0, V0 + NPA + NPB + p)),
        ],
        out_specs=pl.BlockSpec((None, BLK, LANES), lambda bi, p, i: (bi, i, p)),
        out_shape=jax.ShapeDtypeStruct((b, seq, NPC * LANES), jnp.bfloat16),
        compiler_params=pltpu.CompilerParams(
            dimension_semantics=("arbitrary", "arbitrary", "arbitrary"), vmem_limit_bytes=VMEM_LIMIT),
        name="stickbreak_attn",
    )(qkv3, qkv3, qkv3)


def _merge_body(x_ref, mod_ref, g_ref, oa_ref, ob_ref, oc_ref, wa_ref, wb_ref, wc_ref,
                wg_ref, bg_ref, wo_ref, out_ref):
    x = x_ref[...]
    d = x.shape[1]
    h = _bf16(_modulated_norm(x, g_ref[...], mod_ref[0:1, :], mod_ref[1:2, :]))
    merged = None
    for br, (o_ref, w_ref) in enumerate(((oa_ref, wa_ref), (ob_ref, wb_ref), (oc_ref, wc_ref))):
        gate = jax.nn.sigmoid(_dot(h, wg_ref[:, br * d:(br + 1) * d]) + bg_ref[:, br * d:(br + 1) * d])
        term = gate * _dot(o_ref[...], w_ref[...])
        merged = term if merged is None else merged + term
    out_ref[...] = x + mod_ref[2:3, :] * _dot(_bf16(merged), wo_ref[...])


def _merge_call(x2, mod, norm_g, oa, ob, oc, w_br_a, w_br_b, w_br_c, w_gate, b_gate, w_out, layer, seq):
    t, d = x2.shape
    tm = 512
    tiles_per_seq = seq // tm

    def whole(arr):
        return pl.BlockSpec((None,) + arr.shape[1:], lambda i: (layer,) + (0,) * (arr.ndim - 1))

    def rows(arr):
        return pl.BlockSpec((tm, arr.shape[1]), lambda i: (i, 0))

    return pl.pallas_call(
        _merge_body,
        grid=(t // tm,),
        in_specs=[
            rows(x2),
            pl.BlockSpec((None, None, 6, d), lambda i: (layer, i // tiles_per_seq, 0, 0)),
            whole(norm_g), rows(oa), rows(ob), rows(oc),
            whole(w_br_a), whole(w_br_b), whole(w_br_c), whole(w_gate), whole(b_gate), whole(w_out),
        ],
        out_specs=rows(x2),
        out_shape=jax.ShapeDtypeStruct((t, d), jnp.float32),
        compiler_params=pltpu.CompilerParams(
            dimension_semantics=("arbitrary",), vmem_limit_bytes=VMEM_LIMIT),
        name="gated_merge",
    )(x2, mod, norm_g, oa, ob, oc, w_br_a, w_br_b, w_br_c, w_gate, b_gate, w_out)


def _ffn_body(x_ref, mod_ref, g_ref, wg_ref, wu_ref, wd_ref, out_ref, h_ref, acc_ref):
    j = pl.program_id(1)

    @pl.when(j == 0)
    def _():
        h = _modulated_norm(x_ref[...], g_ref[...], mod_ref[3:4, :], mod_ref[4:5, :])
        h_ref[...] = _bf16(h)
        acc_ref[...] = jnp.zeros_like(acc_ref)

    h = h_ref[...]
    gt = _dot(h, wg_ref[...])
    up = _dot(h, wu_ref[...])
    act = (gt * jax.nn.sigmoid(gt)) * up
    acc_ref[...] += _dot(_bf16(act), wd_ref[...])

    @pl.when(j == pl.num_programs(1) - 1)
    def _():
        out_ref[...] = x_ref[...] + mod_ref[5:6, :] * acc_ref[...]


def _ffn_call(x2, mod, norm_g, w_gu, w_down, layer, seq):
    t, d = x2.shape
    d_ff = w_down.shape[1]
    tm, tf = 1024, 256
    n_ff = d_ff // tf
    tiles_per_seq = seq // tm
    return pl.pallas_call(
        _ffn_body,
        grid=(t // tm, n_ff),
        in_specs=[
            pl.BlockSpec((tm, d), lambda i, j: (i, 0)),
            pl.BlockSpec((None, None, 6, d), lambda i, j: (layer, i // tiles_per_seq, 0, 0)),
            pl.BlockSpec((None, 1, d), lambda i, j: (layer, 0, 0)),
            pl.BlockSpec((None, d, tf), lambda i, j: (layer, 0, j)),
            pl.BlockSpec((None, d, tf), lambda i, j: (layer, 0, n_ff + j)),
            pl.BlockSpec((None, tf, d), lambda i, j: (layer, j, 0)),
        ],
        out_specs=pl.BlockSpec((tm, d), lambda i, j: (i, 0)),
        out_shape=jax.ShapeDtypeStruct((t, d), jnp.float32),
        scratch_shapes=[pltpu.VMEM((tm, d), jnp.bfloat16), pltpu.VMEM((tm, d), jnp.float32)],
        compiler_params=pltpu.CompilerParams(
            dimension_semantics=("arbitrary", "arbitrary"), vmem_limit_bytes=VMEM_LIMIT),
        name="swiglu_ffn",
    )(x2, mod, norm_g, w_gu, w_gu, w_down)


def _final_body(x_ref, g_ref, o_ref):
    x = x_ref[...]
    ms = jnp.mean(x * x, axis=-1, keepdims=True)
    o_ref[...] = (x * lax.rsqrt(ms + NORM_EPS)) * g_ref[...]


def _final_call(x2, g):
    t, d = x2.shape
    tm = 1024
    return pl.pallas_call(
        _final_body,
        grid=(t // tm,),
        in_specs=[pl.BlockSpec((tm, d), lambda i: (i, 0)), pl.BlockSpec((1, d), lambda i: (0, 0))],
        out_specs=pl.BlockSpec((tm, d), lambda i: (i, 0)),
        out_shape=jax.ShapeDtypeStruct((t, d), jnp.float32),
        compiler_params=pltpu.CompilerParams(
            dimension_semantics=("arbitrary",), vmem_limit_bytes=VMEM_LIMIT),
        name="final_norm",
    )(x2, g.reshape(1, d))


def _rope_tables(seq):
    pos = jnp.arange(seq, dtype=jnp.float32)
    inv = ROPE_THETA ** (-jnp.arange(0, HEAD_DIM, 2, dtype=jnp.float32) / HEAD_DIM)
    ang = pos[:, None] * inv[None, :]
    cos, sin = jnp.cos(ang), jnp.sin(ang)
    cos_t = jnp.tile(cos, (1, 2 * LANES // HEAD_DIM))
    sin_t = jnp.tile(jnp.concatenate([-sin, sin], axis=-1), (1, LANES // HEAD_DIM))
    return cos_t, sin_t


def _arrange_w_in(w_in):
    n_rot = (HA + HB) * HEAD_DIM
    q = w_in[:, :, :MIX_WIDTH] * (HEAD_DIM ** -0.5)
    k = w_in[:, :, MIX_WIDTH:2 * MIX_WIDTH]
    v = w_in[:, :, 2 * MIX_WIDTH:]
    return _bf16(jnp.concatenate([q[:, :, :n_rot], k[:, :, :n_rot], q[:, :, n_rot:], k[:, :, n_rot:], v],
                                 axis=-1))


def kernel(x, c, w_ada, b_ada, norm1_g, w_in, w_br_a, w_br_b, w_br_c, w_gate, b_gate, w_out,
           norm2_g, w_gu, w_down, final_g):
    b, seq, d = x.shape
    depth = w_in.shape[0]
    cos_t, sin_t = _rope_tables(seq)
    w_qkv = _arrange_w_in(w_in)
    w_br_a, w_br_b, w_br_c = _bf16(w_br_a), _bf16(w_br_b), _bf16(w_br_c)
    w_gate, w_out, w_gu, w_down = _bf16(w_gate), _bf16(w_out), _bf16(w_gu), _bf16(w_down)
    b_gate3 = b_gate.reshape(depth, 1, -1)
    g1 = norm1_g.reshape(depth, 1, d)
    g2 = norm2_g.reshape(depth, 1, d)

    mod = _ada_call(c, w_ada, b_ada).reshape(depth, b, 6, d)
    x2 = x.reshape(b * seq, d)
    for layer in range(depth):
        qkv = _qkv_call(x2, mod, g1, w_qkv, cos_t, sin_t, layer, seq)
        qkv3 = qkv.reshape(b, seq, -1)
        oa = _dil_call(qkv3).reshape(b * seq, -1)
        ob = _moba_call(qkv3).reshape(b * seq, -1)
        oc = _sb_call(qkv3).reshape(b * seq, -1)
        x2 = _merge_call(x2, mod, g1, oa, ob, oc, w_br_a, w_br_b, w_br_c, w_gate, b_gate3, w_out,
                         layer, seq)
        x2 = _ffn_call(x2, mod, g2, w_gu, w_down, layer, seq)
    return _final_call(x2, final_g).reshape(b, seq, d)
```

```python
import functools

import jax
import jax.numpy as jnp
from jax import lax
from jax.experimental import pallas as pl
from jax.experimental.pallas import tpu as pltpu

HEAD_DIM = 64
LANES = 128
DIL_GROUPS = ((128, 1), (512, 4), (2048, 16))
HEADS_PER_DIL = 4
HA = len(DIL_GROUPS) * HEADS_PER_DIL
HB = 6
HC = 6
N_HEADS = HA + HB + HC
MIX_WIDTH = N_HEADS * HEAD_DIM
BAND = 128
BLK = 256
MOBA_TOPK = 3
KMEAN_ROWS = 16
ROPE_THETA = 10000.0
NORM_EPS = 1e-6
NEG = -1e30
LOG2E = 1.4426950408889634
SCORE_SCALE = HEAD_DIM ** -0.5 * LOG2E
VMEM_LIMIT = 56 * 1024 * 1024

NPA, NPB, NPC = HA // 2, HB // 2, HC // 2
QR0 = 0
KR0 = NPA + NPB
QC0 = 2 * (NPA + NPB)
KC0 = QC0 + NPC
V0 = KC0 + NPC
N_ROPE_COLS = 2 * (NPA + NPB) * LANES
MXU_COLS = 256


def _f32(x):
    return x.astype(jnp.float32)


def _bf16(x):
    return x.astype(jnp.bfloat16)


def _dot(a, b):
    return jnp.dot(a, b, preferred_element_type=jnp.float32)


def _dot_nt(a, b):
    return lax.dot_general(a, b, (((1,), (1,)), ((), ())), preferred_element_type=jnp.float32)


def _log2(x):
    return jnp.log(x) * LOG2E


def _iota(shape, dim):
    return lax.broadcasted_iota(jnp.int32, shape, dim)


def _modulated_norm(x, g, shift, scale):
    ms = jnp.mean(x * x, axis=-1, keepdims=True)
    y = x * lax.rsqrt(ms + NORM_EPS)
    return (y * g) * (1.0 + scale) + shift


def _qk_head_lo(shape):
    return (_iota(shape, 1) % HEAD_DIM) < (HEAD_DIM // 2)


def _split_heads(q):
    head_lo = _qk_head_lo(q.shape)
    zero = jnp.zeros_like(q)
    return jnp.where(head_lo, q, zero), jnp.where(head_lo, zero, q)


def _ada_body(c_ref, w_ref, b_ref, o_ref):
    c = c_ref[...]
    ca = c * jax.nn.sigmoid(c)
    o_ref[...] = jnp.dot(ca, w_ref[...], precision=lax.Precision.HIGHEST,
                         preferred_element_type=jnp.float32) + b_ref[...]


def _ada_call(c, w_ada, b_ada):
    depth, d, n = w_ada.shape
    b = c.shape[0]
    tn = 1536
    return pl.pallas_call(
        _ada_body,
        grid=(depth, n // tn),
        in_specs=[
            pl.BlockSpec((b, d), lambda l, j: (0, 0)),
            pl.BlockSpec((None, d, tn), lambda l, j: (l, 0, j)),
            pl.BlockSpec((None, 1, tn), lambda l, j: (l, 0, j)),
        ],
        out_specs=pl.BlockSpec((None, b, tn), lambda l, j: (l, 0, j)),
        out_shape=jax.ShapeDtypeStruct((depth, b, n), jnp.float32),
        compiler_params=pltpu.CompilerParams(
            dimension_semantics=("arbitrary", "arbitrary"), vmem_limit_bytes=VMEM_LIMIT),
        name="ada_mod",
    )(c, w_ada, b_ada.reshape(depth, 1, n))


def _qkv_body(x_ref, mod_ref, g_ref, w_ref, cos_ref, sin_ref, o_ref):
    h = _bf16(_modulated_norm(x_ref[...], g_ref[...], mod_ref[0:1, :], mod_ref[1:2, :]))
    cos = cos_ref[...]
    sin = sin_ref[...]
    for n in range(w_ref.shape[1] // MXU_COLS):
        res = _dot(h, w_ref[:, n * MXU_COLS:(n + 1) * MXU_COLS])
        for g in range(MXU_COLS // LANES):
            lo = n * MXU_COLS + g * LANES
            blk = res[:, g * LANES:(g + 1) * LANES]
            if lo < N_ROPE_COLS:
                blk = blk * cos + pltpu.roll(blk, HEAD_DIM, 1) * sin
            o_ref[:, lo:lo + LANES] = _bf16(blk)


def _qkv_call(x2, mod, norm_g, w_qkv, cos_t, sin_t, layer, seq):
    t, d = x2.shape
    n = w_qkv.shape[2]
    tm = 512
    tiles_per_seq = seq // tm
    return pl.pallas_call(
        _qkv_body,
        grid=(t // tm,),
        in_specs=[
            pl.BlockSpec((tm, d), lambda i: (i, 0)),
            pl.BlockSpec((None, None, 6, d), lambda i: (layer, i // tiles_per_seq, 0, 0)),
            pl.BlockSpec((None, 1, d), lambda i: (layer, 0, 0)),
            pl.BlockSpec((None, d, n), lambda i: (layer, 0, 0), pipeline_mode=pl.Buffered(1)),
            pl.BlockSpec((tm, LANES), lambda i: (i % tiles_per_seq, 0)),
            pl.BlockSpec((tm, LANES), lambda i: (i % tiles_per_seq, 0)),
        ],
        out_specs=pl.BlockSpec((tm, n), lambda i: (i, 0)),
        out_shape=jax.ShapeDtypeStruct((t, n), jnp.bfloat16),
        compiler_params=pltpu.CompilerParams(
            dimension_semantics=("arbitrary",), vmem_limit_bytes=VMEM_LIMIT),
        name="qkv_proj",
    )(x2, mod, norm_g, w_qkv, cos_t, sin_t)


def _band_block(q, k, v, mask, head_lo):
    outs, lses = [], []
    for qh in _split_heads(q):
        s = jnp.where(mask, _dot_nt(qh, k), NEG)
        m = jnp.max(s, axis=-1, keepdims=True)
        e = jnp.exp2(s - m)
        den = jnp.sum(e, axis=-1, keepdims=True)
        outs.append(_dot(_bf16(e), v) / den)
        lses.append(m + _log2(den))
    o = jnp.where(head_lo, outs[0], outs[1])
    lse = jnp.where(head_lo, lses[0], lses[1])
    return o, lse


def _dil_body(q1, k1, v1, q2, k2, v2, q3, k3, v3, o_ref, f_ref, og_ref, lg_ref, *, seq):
    head_lo = _iota((BAND, LANES), 1) < HEAD_DIM
    row = _iota((BAND, 2 * BAND), 0)
    col = _iota((BAND, 2 * BAND), 1)
    mask_win = (col >= row) & (col <= row + BAND)
    mask_own = _iota((BAND, BAND), 1) <= _iota((BAND, BAND), 0)

    srcs = ((q1, k1, v1), (q2, k2, v2), (q3, k3, v3))
    for g, (_, dil) in enumerate(DIL_GROUPS):
        for a in range(3):
            f_ref[a] = _f32(srcs[g][a][...])
        sub_len = seq // dil
        for r in range(dil):
            qs, ks, vs = (_bf16(f_ref[a, pl.ds(r, sub_len, stride=dil), :]) for a in range(3))
            o_blocks, lse_blocks = [], []
            for qb in range(sub_len // BAND):
                q = qs[qb * BAND:(qb + 1) * BAND]
                if qb == 0:
                    o, lse = _band_block(q, ks[:BAND], vs[:BAND], mask_own, head_lo)
                else:
                    o, lse = _band_block(q, ks[(qb - 1) * BAND:(qb + 1) * BAND],
                                         vs[(qb - 1) * BAND:(qb + 1) * BAND], mask_win, head_lo)
                o_blocks.append(o)
                lse_blocks.append(lse)
            og_ref[g, pl.ds(r, sub_len, stride=dil), :] = jnp.concatenate(o_blocks, axis=0)
            lg_ref[g, pl.ds(r, sub_len, stride=dil), :] = jnp.concatenate(lse_blocks, axis=0)

    l1, l2, l3 = lg_ref[0], lg_ref[1], lg_ref[2]
    mx = jnp.maximum(jnp.maximum(l1, l2), l3)
    w1, w2, w3 = jnp.exp2(l1 - mx), jnp.exp2(l2 - mx), jnp.exp2(l3 - mx)
    o = (w1 * og_ref[0] + w2 * og_ref[1] + w3 * og_ref[2]) / (w1 + w2 + w3)
    o_ref[...] = _bf16(o)


def _dil_call(qkv3):
    b, seq, _ = qkv3.shape
    n_slot_pairs = HEADS_PER_DIL // 2

    def spec(base, g):
        return pl.BlockSpec((None, seq, LANES), lambda bi, p: (bi, 0, base + g * n_slot_pairs + p))

    in_specs = []
    for g in range(len(DIL_GROUPS)):
        in_specs += [spec(QR0, g), spec(KR0, g), spec(V0, g)]
    return pl.pallas_call(
        functools.partial(_dil_body, seq=seq),
        grid=(b, n_slot_pairs),
        in_specs=in_specs,
        out_specs=pl.BlockSpec((None, seq, LANES), lambda bi, p: (bi, 0, p)),
        out_shape=jax.ShapeDtypeStruct((b, seq, n_slot_pairs * LANES), jnp.bfloat16),
        scratch_shapes=[
            pltpu.VMEM((3, seq, LANES), jnp.float32),
            pltpu.VMEM((len(DIL_GROUPS), seq, LANES), jnp.float32),
            pltpu.VMEM((len(DIL_GROUPS), seq, LANES), jnp.float32),
        ],
        compiler_params=pltpu.CompilerParams(
            dimension_semantics=("arbitrary", "arbitrary"), vmem_limit_bytes=VMEM_LIMIT),
        name="dilated_attn",
    )(*([qkv3] * 9))


def _moba_prepare(q_ref, k_ref, v_ref, kmean_ref, qa_ref, ka_ref, va_ref, n_blk, first):
    v = v_ref[...]
    v_lo = _iota(v.shape, 1) < HEAD_DIM
    one = jnp.ones_like(v)
    va_ref[0] = jnp.where(v_lo, v, one)
    va_ref[1] = jnp.where(v_lo, one, v)
    if first >= n_blk:
        return

    kmean_ref[...] = jnp.zeros_like(kmean_ref)
    for j in range(n_blk):
        kj = _f32(k_ref[j * BLK:(j + 1) * BLK, :])
        kmean_ref[j:j + 1, :] = jnp.mean(kj, axis=0, keepdims=True)
    km = kmean_ref[...]
    km_hi = _bf16(km)
    km_mid = _bf16(km - _f32(km_hi))
    km_lo = _bf16(km - _f32(km_hi) - _f32(km_mid))

    q = q_ref[first * BLK:, :]
    k = k_ref[...]
    nq = q.shape[0]
    q_lo = _qk_head_lo(q.shape)
    k_lo = _qk_head_lo(k.shape)
    lane = _iota(k.shape, 1)
    key_blk = _iota(k.shape, 0) // BLK
    blk_id = _iota((KMEAN_ROWS, nq), 0)
    past = blk_id < _iota((KMEAN_ROWS, nq), 1) // BLK + first
    zero = jnp.zeros_like(q)
    pad = jnp.zeros((LANES - KMEAN_ROWS, nq), jnp.float32)
    for h in range(2):
        own = q_lo if h == 0 else ~q_lo
        qh = jnp.where(own, q, zero)
        gate = _dot_nt(km_hi, qh) + _dot_nt(km_mid, qh) + _dot_nt(km_lo, qh)
        sel = jnp.zeros(gate.shape, jnp.float32)
        for j in range(n_blk - 1):
            gj = gate[j:j + 1, :]
            beats = ((gate > gj) | ((gate == gj) & (blk_id < j))) & past
            rank = jnp.sum(jnp.where(beats, 1.0, 0.0), axis=0, keepdims=True)
            sel = jnp.where((blk_id == j) & (rank < MOBA_TOPK), 1.0, sel)
        bias = jnp.where(past & (sel < 0.5), NEG, 0.0)
        bias_t = jnp.concatenate([bias, pad], axis=0).T
        base = (1 - h) * (HEAD_DIM // 2)
        if base:
            bias_t = pltpu.roll(bias_t, base, 1)
        qa_ref[h] = jnp.where(own, q, _bf16(bias_t))
        ka_ref[h] = jnp.where(k_lo if h == 0 else ~k_lo, k,
                              jnp.where(lane - base == key_blk, 1.0, 0.0).astype(k.dtype))


def _moba_block(c, first, q_ref, k_ref, qa_ref, ka_ref, va_ref, o_ref):
    n = (c + 1) * BLK
    head_lo = _iota((BLK, LANES), 1) < HEAD_DIM
    causal = _iota((BLK, BLK), 1) <= _iota((BLK, BLK), 0)
    if c < first:
        qs = _split_heads(q_ref[c * BLK:n, :])
        ks = (k_ref[0:n, :],) * 2
    else:
        qs = tuple(qa_ref[h, (c - first) * BLK:(c - first + 1) * BLK, :] for h in range(2))
        ks = tuple(ka_ref[h, 0:n, :] for h in range(2))
    outs = []
    for h in range(2):
        s = _dot_nt(qs[h], ks[h])
        own = jnp.where(causal, s[:, c * BLK:], NEG)
        m = jnp.max(own, axis=-1, keepdims=True)
        if c > 0:
            left = s[:, :c * BLK]
            m = jnp.maximum(m, jnp.max(left, axis=-1, keepdims=True))
            p = jnp.concatenate([_bf16(jnp.exp2(left - m)), _bf16(jnp.exp2(own - m))], axis=1)
        else:
            p = _bf16(jnp.exp2(own - m))
        acc = _dot(p, va_ref[h, 0:n, :])
        outs.append(acc / pltpu.roll(acc, HEAD_DIM, 1))
    o_ref[c * BLK:n, :] = _bf16(jnp.where(head_lo, outs[0], outs[1]))


def _moba_body(q_ref, k_ref, v_ref, o_ref, kmean_ref, qa_ref, ka_ref, va_ref, *, n_blk, first):
    _moba_prepare(q_ref, k_ref, v_ref, kmean_ref, qa_ref, ka_ref, va_ref, n_blk, first)
    for c in range(n_blk):
        _moba_block(c, first, q_ref, k_ref, qa_ref, ka_ref, va_ref, o_ref)


def _moba_call(qkv3):
    b, seq, _ = qkv3.shape
    n_blk = seq // BLK
    first = min(MOBA_TOPK + 1, n_blk)
    n_sel = max(seq - first * BLK, BLK)
    return pl.pallas_call(
        functools.partial(_moba_body, n_blk=n_blk, first=first),
        grid=(b, NPB),
        in_specs=[
            pl.BlockSpec((None, seq, LANES), lambda bi, p: (bi, 0, QR0 + NPA + p)),
            pl.BlockSpec((None, seq, LANES), lambda bi, p: (bi, 0, KR0 + NPA + p)),
            pl.BlockSpec((None, seq, LANES), lambda bi, p: (bi, 0, V0 + NPA + p)),
        ],
        out_specs=pl.BlockSpec((None, seq, LANES), lambda bi, p: (bi, 0, p)),
        out_shape=jax.ShapeDtypeStruct((b, seq, NPB * LANES), jnp.bfloat16),
        scratch_shapes=[pltpu.VMEM((KMEAN_ROWS, LANES), jnp.float32),
                        pltpu.VMEM((2, n_sel, LANES), jnp.bfloat16),
                        pltpu.VMEM((2, seq, LANES), jnp.bfloat16),
                        pltpu.VMEM((2, seq, LANES), jnp.bfloat16)],
        compiler_params=pltpu.CompilerParams(
            dimension_semantics=("arbitrary", "arbitrary"), vmem_limit_bytes=VMEM_LIMIT),
        name="moba_attn",
    )(qkv3, qkv3, qkv3)


def _sb_block(c, q_ref, k_ref, v_ref, o_ref):
    n = (c + 1) * BLK
    qhs = _split_heads(q_ref[c * BLK:n, :])
    head_lo = _iota((BLK, LANES), 1) < HEAD_DIM
    row = _iota((BLK, BLK), 0)
    col = _iota((BLK, BLK), 1)
    past = col < row
    from_s = jnp.where(row >= col, 1.0, 0.0).astype(jnp.bfloat16)
    from_s2 = jnp.concatenate([from_s, from_s], axis=0)
    k = k_ref[0:n, :]
    v = v_ref[0:n, :]
    accs = []
    for h in range(2):
        z_all = _dot_nt(qhs[h], k)
        z_parts, tail_parts = [], []
        for j in range(c + 1):
            z = z_all[:, j * BLK:(j + 1) * BLK]
            sp = jnp.maximum(z, 0.0) + _log2(1.0 + jnp.exp2(-jnp.abs(z)))
            if j == c:
                sp = jnp.where(past, sp, 0.0)
            hi = _bf16(sp)
            lo = _bf16(sp - _f32(hi))
            tail_parts.append(_dot(jnp.concatenate([hi, lo], axis=1), from_s2))
            z_parts.append(z)
        a_parts = [None] * (c + 1)
        beyond = None
        for j in range(c, -1, -1):
            t = z_parts[j] - tail_parts[j]
            if beyond is not None:
                t = t - beyond
            a = jnp.exp2(t)
            if j == c:
                a = jnp.where(past, a, 0.0)
            a_parts[j] = _bf16(a)
            total = tail_parts[j][:, 0:1]
            beyond = total if beyond is None else beyond + total
        accs.append(_dot(jnp.concatenate(a_parts, axis=1), v))
    o_ref[c * BLK:n, :] = _bf16(jnp.where(head_lo, accs[0], accs[1]))


def _sb_body(q_ref, k_ref, v_ref, o_ref, *, n_blk):
    i = pl.program_id(2)
    for s in range(n_blk // 2):
        @pl.when(i == s)
        def _(s=s):
            _sb_block(s, q_ref, k_ref, v_ref, o_ref)
            _sb_block(n_blk - 1 - s, q_ref, k_ref, v_ref, o_ref)


def _sb_call(qkv3):
    b, seq, _ = qkv3.shape
    n_blk = seq // BLK
    return pl.pallas_call(
        functools.partial(_sb_body, n_blk=n_blk),
        grid=(b, NPC, n_blk // 2),
        in_specs=[
            pl.BlockSpec((None, seq, LANES), lambda bi, p, i: (bi, 0, QC0 + p)),
            pl.BlockSpec((None, seq, LANES), lambda bi, p, i: (bi, 0, KC0 + p)),
            pl.BlockSpec((None, seq, LANES), lambda bi, p, i: (bi, 0, V0 + NPA + NPB + p)),
        ],
        out_specs=pl.BlockSpec((None, seq, LANES), lambda bi, p, i: (bi, 0, p)),
        out_shape=jax.ShapeDtypeStruct((b, seq, NPC * LANES), jnp.bfloat16),
        compiler_params=pltpu.CompilerParams(
            dimension_semantics=("arbitrary", "arbitrary", "arbitrary"), vmem_limit_bytes=VMEM_LIMIT),
        name="stickbreak_attn",
    )(qkv3, qkv3, qkv3)


def _merge_body(x_ref, mod_ref, g_ref, oa_ref, ob_ref, oc_ref, wa_ref, wb_ref, wc_ref,
                wg_ref, bg_ref, wo_ref, out_ref):
    x = x_ref[...]
    d = x.shape[1]
    h = _bf16(_modulated_norm(x, g_ref[...], mod_ref[0:1, :], mod_ref[1:2, :]))
    merged = None
    for br, (o_ref, w_ref) in enumerate(((oa_ref, wa_ref), (ob_ref, wb_ref), (oc_ref, wc_ref))):
        gate = jax.nn.sigmoid(_dot(h, wg_ref[:, br * d:(br + 1) * d]) + bg_ref[:, br * d:(br + 1) * d])
        term = gate * _dot(o_ref[...], w_ref[...])
        merged = term if merged is None else merged + term
    out_ref[...] = x + mod_ref[2:3, :] * _dot(_bf16(merged), wo_ref[...])


def _merge_call(x2, mod, norm_g, oa, ob, oc, w_br_a, w_br_b, w_br_c, w_gate, b_gate, w_out, layer, seq):
    t, d = x2.shape
    tm = 512
    tiles_per_seq = seq // tm

    def whole(arr):
        return pl.BlockSpec((None,) + arr.shape[1:], lambda i: (layer,) + (0,) * (arr.ndim - 1),
                            pipeline_mode=pl.Buffered(1))

    def rows(arr):
        return pl.BlockSpec((tm, arr.shape[1]), lambda i: (i, 0))

    return pl.pallas_call(
        _merge_body,
        grid=(t // tm,),
        in_specs=[
            rows(x2),
            pl.BlockSpec((None, None, 6, d), lambda i: (layer, i // tiles_per_seq, 0, 0)),
            whole(norm_g), rows(oa), rows(ob), rows(oc),
            whole(w_br_a), whole(w_br_b), whole(w_br_c), whole(w_gate), whole(b_gate), whole(w_out),
        ],
        out_specs=rows(x2),
        out_shape=jax.ShapeDtypeStruct((t, d), jnp.float32),
        compiler_params=pltpu.CompilerParams(
            dimension_semantics=("arbitrary",), vmem_limit_bytes=VMEM_LIMIT),
        name="gated_merge",
    )(x2, mod, norm_g, oa, ob, oc, w_br_a, w_br_b, w_br_c, w_gate, b_gate, w_out)


def _ffn_body(x_ref, mod_ref, g_ref, wg_ref, wu_ref, wd_ref, out_ref):
    x = x_ref[...]
    h = _bf16(_modulated_norm(x, g_ref[...], mod_ref[3:4, :], mod_ref[4:5, :]))
    gt = _dot(h, wg_ref[...])
    up = _dot(h, wu_ref[...])
    act = (gt * jax.nn.sigmoid(gt)) * up
    out_ref[...] = x + mod_ref[5:6, :] * _dot(_bf16(act), wd_ref[...])


def _ffn_call(x2, mod, norm_g, w_gu, w_down, layer, seq):
    t, d = x2.shape
    d_ff = w_down.shape[1]
    tm = 512
    tiles_per_seq = seq // tm
    resident = pl.Buffered(1)
    return pl.pallas_call(
        _ffn_body,
        grid=(t // tm,),
        in_specs=[
            pl.BlockSpec((tm, d), lambda i: (i, 0)),
            pl.BlockSpec((None, None, 6, d), lambda i: (layer, i // tiles_per_seq, 0, 0)),
            pl.BlockSpec((None, 1, d), lambda i: (layer, 0, 0)),
            pl.BlockSpec((None, d, d_ff), lambda i: (layer, 0, 0), pipeline_mode=resident),
            pl.BlockSpec((None, d, d_ff), lambda i: (layer, 0, 1), pipeline_mode=resident),
            pl.BlockSpec((None, d_ff, d), lambda i: (layer, 0, 0), pipeline_mode=resident),
        ],
        out_specs=pl.BlockSpec((tm, d), lambda i: (i, 0)),
        out_shape=jax.ShapeDtypeStruct((t, d), jnp.float32),
        compiler_params=pltpu.CompilerParams(
            dimension_semantics=("arbitrary",), vmem_limit_bytes=VMEM_LIMIT),
        name="swiglu_ffn",
    )(x2, mod, norm_g, w_gu, w_gu, w_down)


def _final_body(x_ref, g_ref, o_ref):
    x = x_ref[...]
    ms = jnp.mean(x * x, axis=-1, keepdims=True)
    o_ref[...] = (x * lax.rsqrt(ms + NORM_EPS)) * g_ref[...]


def _final_call(x2, g):
    t, d = x2.shape
    tm = 1024
    return pl.pallas_call(
        _final_body,
        grid=(t // tm,),
        in_specs=[pl.BlockSpec((tm, d), lambda i: (i, 0)), pl.BlockSpec((1, d), lambda i: (0, 0))],
        out_specs=pl.BlockSpec((tm, d), lambda i: (i, 0)),
        out_shape=jax.ShapeDtypeStruct((t, d), jnp.float32),
        compiler_params=pltpu.CompilerParams(
            dimension_semantics=("arbitrary",), vmem_limit_bytes=VMEM_LIMIT),
        name="final_norm",
    )(x2, g.reshape(1, d))


def _rope_tables(seq):
    pos = jnp.arange(seq, dtype=jnp.float32)
    inv = ROPE_THETA ** (-jnp.arange(0, HEAD_DIM, 2, dtype=jnp.float32) / HEAD_DIM)
    ang = pos[:, None] * inv[None, :]
    cos, sin = jnp.cos(ang), jnp.sin(ang)
    cos_t = jnp.tile(cos, (1, 2 * LANES // HEAD_DIM))
    sin_t = jnp.concatenate([-sin, -sin, sin, sin], axis=-1)
    return cos_t, sin_t


def _pair_layout(w):
    lead = w.shape[:-1]
    w = w.reshape(*lead, -1, 2, 2, HEAD_DIM // 2)
    return jnp.swapaxes(w, -2, -3).reshape(*lead, -1)


def _arrange_w_in(w_in):
    n_rot = (HA + HB) * HEAD_DIM
    q = _pair_layout(w_in[:, :, :MIX_WIDTH] * SCORE_SCALE)
    k = _pair_layout(w_in[:, :, MIX_WIDTH:2 * MIX_WIDTH])
    v = w_in[:, :, 2 * MIX_WIDTH:]
    return _bf16(jnp.concatenate([q[:, :, :n_rot], k[:, :, :n_rot], q[:, :, n_rot:], k[:, :, n_rot:], v],
                                 axis=-1))


def kernel(x, c, w_ada, b_ada, norm1_g, w_in, w_br_a, w_br_b, w_br_c, w_gate, b_gate, w_out,
           norm2_g, w_gu, w_down, final_g):
    b, seq, d = x.shape
    depth = w_in.shape[0]
    cos_t, sin_t = _rope_tables(seq)
    w_qkv = _arrange_w_in(w_in)
    w_br_a, w_br_b, w_br_c = _bf16(w_br_a), _bf16(w_br_b), _bf16(w_br_c)
    w_gate, w_out, w_gu, w_down = _bf16(w_gate), _bf16(w_out), _bf16(w_gu), _bf16(w_down)
    b_gate3 = b_gate.reshape(depth, 1, -1)
    g1 = norm1_g.reshape(depth, 1, d)
    g2 = norm2_g.reshape(depth, 1, d)

    mod = _ada_call(c, w_ada, b_ada).reshape(depth, b, 6, d)
    x2 = x.reshape(b * seq, d)
    for layer in range(depth):
        qkv = _qkv_call(x2, mod, g1, w_qkv, cos_t, sin_t, layer, seq)
        qkv3 = qkv.reshape(b, seq, -1)
        oa = _dil_call(qkv3).reshape(b * seq, -1)
        ob = _moba_call(qkv3).reshape(b * seq, -1)
        oc = _sb_call(qkv3).reshape(b * seq, -1)
        x2 = _merge_call(x2, mod, g1, oa, ob, oc, w_br_a, w_br_b, w_br_c, w_gate, b_gate3, w_out,
                         layer, seq)
        x2 = _ffn_call(x2, mod, g2, w_gu, w_down, layer, seq)
    return _final_call(x2, final_g).reshape(b, seq, d)
```

```python
import functools

import jax
import jax.numpy as jnp
from jax import lax
from jax.experimental import pallas as pl
from jax.experimental.pallas import tpu as pltpu

HEAD_DIM = 64
LANES = 128
DIL_GROUPS = ((128, 1), (512, 4), (2048, 16))
HEADS_PER_DIL = 4
HA = len(DIL_GROUPS) * HEADS_PER_DIL
HB = 6
HC = 6
N_HEADS = HA + HB + HC
MIX_WIDTH = N_HEADS * HEAD_DIM
BAND = 128
DIL_AHEAD = 4
BLK = 256
MOBA_TOPK = 3
KMEAN_ROWS = 16
ROPE_THETA = 10000.0
NORM_EPS = 1e-6
NEG = -1e30
LOG2E = 1.4426950408889634
SCORE_SCALE = HEAD_DIM ** -0.5 * LOG2E
VMEM_LIMIT = 56 * 1024 * 1024

NPA, NPB, NPC = HA // 2, HB // 2, HC // 2
QR0 = 0
KR0 = NPA + NPB
QC0 = 2 * (NPA + NPB)
KC0 = QC0 + NPC
V0 = KC0 + NPC
N_ROPE_COLS = 2 * (NPA + NPB) * LANES
MXU_COLS = 256


def _f32(x):
    return x.astype(jnp.float32)


def _bf16(x):
    return x.astype(jnp.bfloat16)


def _dot(a, b):
    return jnp.dot(a, b, preferred_element_type=jnp.float32)


def _dot_nt(a, b):
    return lax.dot_general(a, b, (((1,), (1,)), ((), ())), preferred_element_type=jnp.float32)


def _log2(x):
    return jnp.log(x) * LOG2E


def _iota(shape, dim):
    return lax.broadcasted_iota(jnp.int32, shape, dim)


def _modulated_norm(x, g, shift, scale):
    ms = jnp.mean(x * x, axis=-1, keepdims=True)
    y = x * lax.rsqrt(ms + NORM_EPS)
    return (y * g) * (1.0 + scale) + shift


def _qk_head_lo(shape):
    return (_iota(shape, 1) % HEAD_DIM) < (HEAD_DIM // 2)


def _split_heads(q):
    head_lo = _qk_head_lo(q.shape)
    zero = jnp.zeros_like(q)
    return jnp.where(head_lo, q, zero), jnp.where(head_lo, zero, q)


def _ada_body(c_ref, w_ref, b_ref, o_ref):
    c = c_ref[...]
    ca = c * jax.nn.sigmoid(c)
    o_ref[...] = jnp.dot(ca, w_ref[...], precision=lax.Precision.HIGHEST,
                         preferred_element_type=jnp.float32) + b_ref[...]


def _ada_call(c, w_ada, b_ada):
    depth, d, n = w_ada.shape
    b = c.shape[0]
    tn = 1536
    return pl.pallas_call(
        _ada_body,
        grid=(depth, n // tn),
        in_specs=[
            pl.BlockSpec((b, d), lambda l, j: (0, 0)),
            pl.BlockSpec((None, d, tn), lambda l, j: (l, 0, j)),
            pl.BlockSpec((None, 1, tn), lambda l, j: (l, 0, j)),
        ],
        out_specs=pl.BlockSpec((None, b, tn), lambda l, j: (l, 0, j)),
        out_shape=jax.ShapeDtypeStruct((depth, b, n), jnp.float32),
        compiler_params=pltpu.CompilerParams(
            dimension_semantics=("arbitrary", "arbitrary"), vmem_limit_bytes=VMEM_LIMIT),
        name="ada_mod",
    )(c, w_ada, b_ada.reshape(depth, 1, n))


def _qkv_body(x_ref, mod_ref, g_ref, w_ref, cos_ref, sin_ref, o_ref):
    h = _bf16(_modulated_norm(x_ref[...], g_ref[...], mod_ref[0:1, :], mod_ref[1:2, :]))
    cos = cos_ref[...]
    sin = sin_ref[...]
    for n in range(w_ref.shape[1] // MXU_COLS):
        res = _dot(h, w_ref[:, n * MXU_COLS:(n + 1) * MXU_COLS])
        for g in range(MXU_COLS // LANES):
            lo = n * MXU_COLS + g * LANES
            blk = res[:, g * LANES:(g + 1) * LANES]
            if lo < N_ROPE_COLS:
                blk = blk * cos + pltpu.roll(blk, HEAD_DIM, 1) * sin
            o_ref[:, lo:lo + LANES] = _bf16(blk)


def _qkv_call(x2, mod, norm_g, w_qkv, cos_t, sin_t, layer, seq):
    t, d = x2.shape
    n = w_qkv.shape[2]
    tm = 512
    tiles_per_seq = seq // tm
    return pl.pallas_call(
        _qkv_body,
        grid=(t // tm,),
        in_specs=[
            pl.BlockSpec((tm, d), lambda i: (i, 0)),
            pl.BlockSpec((None, None, 6, d), lambda i: (layer, i // tiles_per_seq, 0, 0)),
            pl.BlockSpec((None, 1, d), lambda i: (layer, 0, 0)),
            pl.BlockSpec((None, d, n), lambda i: (layer, 0, 0), pipeline_mode=pl.Buffered(1)),
            pl.BlockSpec((tm, LANES), lambda i: (i % tiles_per_seq, 0)),
            pl.BlockSpec((tm, LANES), lambda i: (i % tiles_per_seq, 0)),
        ],
        out_specs=pl.BlockSpec((tm, n), lambda i: (i, 0)),
        out_shape=jax.ShapeDtypeStruct((t, n), jnp.bfloat16),
        compiler_params=pltpu.CompilerParams(
            dimension_semantics=("arbitrary",), vmem_limit_bytes=VMEM_LIMIT),
        name="qkv_proj",
    )(x2, mod, norm_g, w_qkv, cos_t, sin_t)


def _band_scores(q, k, mask):
    return [jnp.where(mask, _dot_nt(qh, k), NEG) for qh in _split_heads(q)]


def _band_finish(scores, v, head_lo):
    outs, lses = [], []
    for s in scores:
        m = jnp.max(s, axis=-1, keepdims=True)
        e = jnp.exp2(s - m)
        den = jnp.sum(e, axis=-1, keepdims=True)
        outs.append(_dot(_bf16(e), v) / den)
        lses.append(m + _log2(den))
    return jnp.where(head_lo, outs[0], outs[1]), jnp.where(head_lo, lses[0], lses[1])


def _dil_body(q1, k1, v1, q2, k2, v2, q3, k3, v3, o_ref, f_ref, og_ref, lg_ref, *, seq):
    head_lo = _iota((BAND, LANES), 1) < HEAD_DIM
    row = _iota((BAND, 2 * BAND), 0)
    col = _iota((BAND, 2 * BAND), 1)
    mask_win = (col >= row) & (col <= row + BAND)
    mask_own = _iota((BAND, BAND), 1) <= _iota((BAND, BAND), 0)

    srcs = ((q1, k1, v1), (q2, k2, v2), (q3, k3, v3))
    for g in range(len(DIL_GROUPS)):
        for a in range(3):
            f_ref[3 * g + a] = _f32(srcs[g][a][...])

    units = []
    for g, (_, dil) in enumerate(DIL_GROUPS):
        sub_len = seq // dil
        for r in range(dil):
            qs, ks, vs = (_bf16(f_ref[3 * g + a, pl.ds(r, sub_len, stride=dil), :]) for a in range(3))
            for qb in range(sub_len // BAND):
                lo, hi = max(qb - 1, 0) * BAND, (qb + 1) * BAND
                units.append((g, r, qs[qb * BAND:hi], ks[lo:hi], vs[lo:hi], mask_own if qb == 0 else mask_win))

    scores, results = {}, {}
    for step in range(len(units) + DIL_AHEAD):
        if step < len(units):
            _, _, q, k, _, mask = units[step]
            scores[step] = _band_scores(q, k, mask)
        done = step - DIL_AHEAD
        if done >= 0:
            g, r, _, _, v, _ = units[done]
            results.setdefault((g, r), []).append(_band_finish(scores.pop(done), v, head_lo))
    for (g, r), blocks in results.items():
        dil = DIL_GROUPS[g][1]
        og_ref[g, pl.ds(r, seq // dil, stride=dil), :] = jnp.concatenate([o for o, _ in blocks], axis=0)
        lg_ref[g, pl.ds(r, seq // dil, stride=dil), :] = jnp.concatenate([l for _, l in blocks], axis=0)

    l1, l2, l3 = lg_ref[0], lg_ref[1], lg_ref[2]
    mx = jnp.maximum(jnp.maximum(l1, l2), l3)
    w1, w2, w3 = jnp.exp2(l1 - mx), jnp.exp2(l2 - mx), jnp.exp2(l3 - mx)
    o = (w1 * og_ref[0] + w2 * og_ref[1] + w3 * og_ref[2]) / (w1 + w2 + w3)
    o_ref[...] = _bf16(o)


def _dil_call(qkv3):
    b, seq, _ = qkv3.shape
    n_slot_pairs = HEADS_PER_DIL // 2

    def spec(base, g):
        return pl.BlockSpec((None, seq, LANES), lambda bi, p: (bi, 0, base + g * n_slot_pairs + p))

    in_specs = []
    for g in range(len(DIL_GROUPS)):
        in_specs += [spec(QR0, g), spec(KR0, g), spec(V0, g)]
    return pl.pallas_call(
        functools.partial(_dil_body, seq=seq),
        grid=(b, n_slot_pairs),
        in_specs=in_specs,
        out_specs=pl.BlockSpec((None, seq, LANES), lambda bi, p: (bi, 0, p)),
        out_shape=jax.ShapeDtypeStruct((b, seq, n_slot_pairs * LANES), jnp.bfloat16),
        scratch_shapes=[
            pltpu.VMEM((3 * len(DIL_GROUPS), seq, LANES), jnp.float32),
            pltpu.VMEM((len(DIL_GROUPS), seq, LANES), jnp.float32),
            pltpu.VMEM((len(DIL_GROUPS), seq, LANES), jnp.float32),
        ],
        compiler_params=pltpu.CompilerParams(
            dimension_semantics=("arbitrary", "arbitrary"), vmem_limit_bytes=VMEM_LIMIT),
        name="dilated_attn",
    )(*([qkv3] * 9))


def _moba_prepare(q_ref, k_ref, v_ref, kmean_ref, qa_ref, ka_ref, va_ref, n_blk, first):
    v = v_ref[...]
    v_lo = _iota(v.shape, 1) < HEAD_DIM
    one = jnp.ones_like(v)
    va_ref[0] = jnp.where(v_lo, v, one)
    va_ref[1] = jnp.where(v_lo, one, v)
    if first >= n_blk:
        return

    kmean_ref[...] = jnp.zeros_like(kmean_ref)
    for j in range(n_blk):
        kj = _f32(k_ref[j * BLK:(j + 1) * BLK, :])
        kmean_ref[j:j + 1, :] = jnp.mean(kj, axis=0, keepdims=True)
    km = kmean_ref[...]
    km_hi = _bf16(km)
    km_mid = _bf16(km - _f32(km_hi))
    km_lo = _bf16(km - _f32(km_hi) - _f32(km_mid))

    q = q_ref[first * BLK:, :]
    k = k_ref[...]
    nq = q.shape[0]
    q_lo = _qk_head_lo(q.shape)
    k_lo = _qk_head_lo(k.shape)
    lane = _iota(k.shape, 1)
    key_blk = _iota(k.shape, 0) // BLK
    blk_id = _iota((KMEAN_ROWS, nq), 0)
    past = blk_id < _iota((KMEAN_ROWS, nq), 1) // BLK + first
    zero = jnp.zeros_like(q)
    pad = jnp.zeros((LANES - KMEAN_ROWS, nq), jnp.float32)
    for h in range(2):
        own = q_lo if h == 0 else ~q_lo
        qh = jnp.where(own, q, zero)
        gate = _dot_nt(km_hi, qh) + _dot_nt(km_mid, qh) + _dot_nt(km_lo, qh)
        sel = jnp.zeros(gate.shape, jnp.float32)
        for j in range(n_blk - 1):
            gj = gate[j:j + 1, :]
            beats = ((gate > gj) | ((gate == gj) & (blk_id < j))) & past
            rank = jnp.sum(jnp.where(beats, 1.0, 0.0), axis=0, keepdims=True)
            sel = jnp.where((blk_id == j) & (rank < MOBA_TOPK), 1.0, sel)
        bias = jnp.where(past & (sel < 0.5), NEG, 0.0)
        bias_t = jnp.concatenate([bias, pad], axis=0).T
        base = (1 - h) * (HEAD_DIM // 2)
        if base:
            bias_t = pltpu.roll(bias_t, base, 1)
        qa_ref[h] = jnp.where(own, q, _bf16(bias_t))
        ka_ref[h] = jnp.where(k_lo if h == 0 else ~k_lo, k,
                              jnp.where(lane - base == key_blk, 1.0, 0.0).astype(k.dtype))


def _moba_scores(c, h, first, q_ref, k_ref, qa_ref, ka_ref):
    n = (c + 1) * BLK
    if c < first:
        return _dot_nt(_split_heads(q_ref[c * BLK:n, :])[h], k_ref[0:n, :])
    return _dot_nt(qa_ref[h, (c - first) * BLK:(c - first + 1) * BLK, :], ka_ref[h, 0:n, :])


def _moba_finish(c, h, s, va_ref):
    n = (c + 1) * BLK
    causal = _iota((BLK, BLK), 1) <= _iota((BLK, BLK), 0)
    own = jnp.where(causal, s[:, c * BLK:], NEG)
    m = jnp.max(own, axis=-1, keepdims=True)
    if c > 0:
        left = s[:, :c * BLK]
        m = jnp.maximum(m, jnp.max(left, axis=-1, keepdims=True))
        p = jnp.concatenate([_bf16(jnp.exp2(left - m)), _bf16(jnp.exp2(own - m))], axis=1)
    else:
        p = _bf16(jnp.exp2(own - m))
    acc = _dot(p, va_ref[h, 0:n, :])
    return acc / pltpu.roll(acc, HEAD_DIM, 1)


def _moba_body(q_ref, k_ref, v_ref, o_ref, kmean_ref, qa_ref, ka_ref, va_ref, *, n_blk, first):
    _moba_prepare(q_ref, k_ref, v_ref, kmean_ref, qa_ref, ka_ref, va_ref, n_blk, first)
    units = [(c, h) for c in range(n_blk) for h in range(2)]
    outs = {}
    pending = None
    for unit in units + [None]:
        s = None if unit is None else _moba_scores(*unit, first, q_ref, k_ref, qa_ref, ka_ref)
        if pending is not None:
            outs[pending[0]] = _moba_finish(*pending[0], pending[1], va_ref)
        pending = (unit, s)
    head_lo = _iota((BLK, LANES), 1) < HEAD_DIM
    for c in range(n_blk):
        o_ref[c * BLK:(c + 1) * BLK, :] = _bf16(jnp.where(head_lo, outs[c, 0], outs[c, 1]))


def _moba_call(qkv3):
    b, seq, _ = qkv3.shape
    n_blk = seq // BLK
    first = min(MOBA_TOPK + 1, n_blk)
    n_sel = max(seq - first * BLK, BLK)
    return pl.pallas_call(
        functools.partial(_moba_body, n_blk=n_blk, first=first),
        grid=(b, NPB),
        in_specs=[
            pl.BlockSpec((None, seq, LANES), lambda bi, p: (bi, 0, QR0 + NPA + p)),
            pl.BlockSpec((None, seq, LANES), lambda bi, p: (bi, 0, KR0 + NPA + p)),
            pl.BlockSpec((None, seq, LANES), lambda bi, p: (bi, 0, V0 + NPA + p)),
        ],
        out_specs=pl.BlockSpec((None, seq, LANES), lambda bi, p: (bi, 0, p)),
        out_shape=jax.ShapeDtypeStruct((b, seq, NPB * LANES), jnp.bfloat16),
        scratch_shapes=[pltpu.VMEM((KMEAN_ROWS, LANES), jnp.float32),
                        pltpu.VMEM((2, n_sel, LANES), jnp.bfloat16),
                        pltpu.VMEM((2, seq, LANES), jnp.bfloat16),
                        pltpu.VMEM((2, seq, LANES), jnp.bfloat16)],
        compiler_params=pltpu.CompilerParams(
            dimension_semantics=("arbitrary", "arbitrary"), vmem_limit_bytes=VMEM_LIMIT),
        name="moba_attn",
    )(qkv3, qkv3, qkv3)


def _sb_unit(c, h, q_ref, k_ref, v_ref):
    n = (c + 1) * BLK
    row = _iota((BLK, BLK), 0)
    col = _iota((BLK, BLK), 1)
    past = col < row
    from_s = jnp.where(row >= col, 1.0, 0.0).astype(jnp.bfloat16)
    from_s2 = jnp.concatenate([from_s, from_s], axis=0)

    z_all = _dot_nt(_split_heads(q_ref[c * BLK:n, :])[h], k_ref[0:n, :])
    yield None

    z_parts, tail_parts = [], []
    for j in range(c + 1):
        z = z_all[:, j * BLK:(j + 1) * BLK]
        sp = jnp.maximum(z, 0.0) + _log2(1.0 + jnp.exp2(-jnp.abs(z)))
        if j == c:
            sp = jnp.where(past, sp, 0.0)
        hi = _bf16(sp)
        lo = _bf16(sp - _f32(hi))
        tail_parts.append(_dot(jnp.concatenate([hi, lo], axis=1), from_s2))
        z_parts.append(z)
    yield None

    a_parts = [None] * (c + 1)
    beyond = None
    for j in range(c, -1, -1):
        t = z_parts[j] - tail_parts[j]
        if beyond is not None:
            t = t - beyond
        a = jnp.exp2(t)
        if j == c:
            a = jnp.where(past, a, 0.0)
        a_parts[j] = _bf16(a)
        total = tail_parts[j][:, 0:1]
        beyond = total if beyond is None else beyond + total
    yield _dot(jnp.concatenate(a_parts, axis=1), v_ref[0:n, :])


def _sb_body(q_ref, k_ref, v_ref, o_ref, *, n_blk):
    units = [(c, h) for c in range(n_blk) for h in range(2)]
    gens = [_sb_unit(c, h, q_ref, k_ref, v_ref) for c, h in units]
    n_stages = 3
    accs = {}
    for step in range(len(units) + n_stages - 1):
        for stage in range(n_stages):
            u = step - stage
            if 0 <= u < len(units):
                res = next(gens[u])
                if stage == n_stages - 1:
                    accs[units[u]] = res
    head_lo = _iota((BLK, LANES), 1) < HEAD_DIM
    for c in range(n_blk):
        o_ref[c * BLK:(c + 1) * BLK, :] = _bf16(jnp.where(head_lo, accs[c, 0], accs[c, 1]))


def _sb_call(qkv3):
    b, seq, _ = qkv3.shape
    n_blk = seq // BLK
    return pl.pallas_call(
        functools.partial(_sb_body, n_blk=n_blk),
        grid=(b, NPC),
        in_specs=[
            pl.BlockSpec((None, seq, LANES), lambda bi, p: (bi, 0, QC0 + p)),
            pl.BlockSpec((None, seq, LANES), lambda bi, p: (bi, 0, KC0 + p)),
            pl.BlockSpec((None, seq, LANES), lambda bi, p: (bi, 0, V0 + NPA + NPB + p)),
        ],
        out_specs=pl.BlockSpec((None, seq, LANES), lambda bi, p: (bi, 0, p)),
        out_shape=jax.ShapeDtypeStruct((b, seq, NPC * LANES), jnp.bfloat16),
        compiler_params=pltpu.CompilerParams(
            dimension_semantics=("arbitrary", "arbitrary"), vmem_limit_bytes=VMEM_LIMIT),
        name="stickbreak_attn",
    )(qkv3, qkv3, qkv3)


def _merge_body(x_ref, mod_ref, g_ref, oa_ref, ob_ref, oc_ref, wa_ref, wb_ref, wc_ref,
                wg_ref, bg_ref, wo_ref, out_ref):
    x = x_ref[...]
    d = x.shape[1]
    h = _bf16(_modulated_norm(x, g_ref[...], mod_ref[0:1, :], mod_ref[1:2, :]))
    merged = None
    for br, (o_ref, w_ref) in enumerate(((oa_ref, wa_ref), (ob_ref, wb_ref), (oc_ref, wc_ref))):
        gate = jax.nn.sigmoid(_dot(h, wg_ref[:, br * d:(br + 1) * d]) + bg_ref[:, br * d:(br + 1) * d])
        term = gate * _dot(o_ref[...], w_ref[...])
        merged = term if merged is None else merged + term
    out_ref[...] = x + mod_ref[2:3, :] * _dot(_bf16(merged), wo_ref[...])


def _merge_call(x2, mod, norm_g, oa, ob, oc, w_br_a, w_br_b, w_br_c, w_gate, b_gate, w_out, layer, seq):
    t, d = x2.shape
    tm = 512
    tiles_per_seq = seq // tm

    def whole(arr):
        return pl.BlockSpec((None,) + arr.shape[1:], lambda i: (layer,) + (0,) * (arr.ndim - 1),
                            pipeline_mode=pl.Buffered(1))

    def rows(arr):
        return pl.BlockSpec((tm, arr.shape[1]), lambda i: (i, 0))

    return pl.pallas_call(
        _merge_body,
        grid=(t // tm,),
        in_specs=[
            rows(x2),
            pl.BlockSpec((None, None, 6, d), lambda i: (layer, i // tiles_per_seq, 0, 0)),
            whole(norm_g), rows(oa), rows(ob), rows(oc),
            whole(w_br_a), whole(w_br_b), whole(w_br_c), whole(w_gate), whole(b_gate), whole(w_out),
        ],
        out_specs=rows(x2),
        out_shape=jax.ShapeDtypeStruct((t, d), jnp.float32),
        compiler_params=pltpu.CompilerParams(
            dimension_semantics=("arbitrary",), vmem_limit_bytes=VMEM_LIMIT),
        name="gated_merge",
    )(x2, mod, norm_g, oa, ob, oc, w_br_a, w_br_b, w_br_c, w_gate, b_gate, w_out)


def _rms_norm(x, g):
    ms = jnp.mean(x * x, axis=-1, keepdims=True)
    return (x * lax.rsqrt(ms + NORM_EPS)) * g


def _ffn_body(x_ref, mod_ref, g_ref, wg_ref, wu_ref, wd_ref, *rest):
    maybe_final_g_ref, out_ref = rest[:-1], rest[-1]
    x = x_ref[...]
    h = _bf16(_modulated_norm(x, g_ref[...], mod_ref[3:4, :], mod_ref[4:5, :]))
    gt = _dot(h, wg_ref[...])
    up = _dot(h, wu_ref[...])
    act = (gt * jax.nn.sigmoid(gt)) * up
    y = x + mod_ref[5:6, :] * _dot(_bf16(act), wd_ref[...])
    for final_g_ref in maybe_final_g_ref:
        y = _rms_norm(y, final_g_ref[...])
    out_ref[...] = y


def _ffn_call(x2, mod, norm_g, w_gu, w_down, layer, seq, final_g=None):
    t, d = x2.shape
    d_ff = w_down.shape[1]
    tm = 512
    tiles_per_seq = seq // tm
    resident = pl.Buffered(1)
    in_specs = [
        pl.BlockSpec((tm, d), lambda i: (i, 0)),
        pl.BlockSpec((None, None, 6, d), lambda i: (layer, i // tiles_per_seq, 0, 0)),
        pl.BlockSpec((None, 1, d), lambda i: (layer, 0, 0)),
        pl.BlockSpec((None, d, d_ff), lambda i: (layer, 0, 0), pipeline_mode=resident),
        pl.BlockSpec((None, d, d_ff), lambda i: (layer, 0, 1), pipeline_mode=resident),
        pl.BlockSpec((None, d_ff, d), lambda i: (layer, 0, 0), pipeline_mode=resident),
    ]
    args = [x2, mod, norm_g, w_gu, w_gu, w_down]
    if final_g is not None:
        in_specs.append(pl.BlockSpec((1, d), lambda i: (0, 0)))
        args.append(final_g.reshape(1, d))
    return pl.pallas_call(
        _ffn_body,
        grid=(t // tm,),
        in_specs=in_specs,
        out_specs=pl.BlockSpec((tm, d), lambda i: (i, 0)),
        out_shape=jax.ShapeDtypeStruct((t, d), jnp.float32),
        compiler_params=pltpu.CompilerParams(
            dimension_semantics=("arbitrary",), vmem_limit_bytes=VMEM_LIMIT),
        name="swiglu_ffn",
    )(*args)


def _rope_tables(seq):
    pos = jnp.arange(seq, dtype=jnp.float32)
    inv = ROPE_THETA ** (-jnp.arange(0, HEAD_DIM, 2, dtype=jnp.float32) / HEAD_DIM)
    ang = pos[:, None] * inv[None, :]
    cos, sin = jnp.cos(ang), jnp.sin(ang)
    cos_t = jnp.tile(cos, (1, 2 * LANES // HEAD_DIM))
    sin_t = jnp.concatenate([-sin, -sin, sin, sin], axis=-1)
    return cos_t, sin_t


def _pair_layout(w):
    lead = w.shape[:-1]
    w = w.reshape(*lead, -1, 2, 2, HEAD_DIM // 2)
    return jnp.swapaxes(w, -2, -3).reshape(*lead, -1)


def _arrange_w_in(w_in):
    n_rot = (HA + HB) * HEAD_DIM
    q = _pair_layout(w_in[:, :, :MIX_WIDTH] * SCORE_SCALE)
    k = _pair_layout(w_in[:, :, MIX_WIDTH:2 * MIX_WIDTH])
    v = w_in[:, :, 2 * MIX_WIDTH:]
    return _bf16(jnp.concatenate([q[:, :, :n_rot], k[:, :, :n_rot], q[:, :, n_rot:], k[:, :, n_rot:], v],
                                 axis=-1))


def kernel(x, c, w_ada, b_ada, norm1_g, w_in, w_br_a, w_br_b, w_br_c, w_gate, b_gate, w_out,
           norm2_g, w_gu, w_down, final_g):
    b, seq, d = x.shape
    depth = w_in.shape[0]
    cos_t, sin_t = _rope_tables(seq)
    w_qkv = _arrange_w_in(w_in)
    w_br_a, w_br_b, w_br_c = _bf16(w_br_a), _bf16(w_br_b), _bf16(w_br_c)
    w_gate, w_out, w_gu, w_down = _bf16(w_gate), _bf16(w_out), _bf16(w_gu), _bf16(w_down)
    b_gate3 = b_gate.reshape(depth, 1, -1)
    g1 = norm1_g.reshape(depth, 1, d)
    g2 = norm2_g.reshape(depth, 1, d)

    mod = _ada_call(c, w_ada, b_ada).reshape(depth, b, 6, d)
    x2 = x.reshape(b * seq, d)
    for layer in range(depth):
        qkv = _qkv_call(x2, mod, g1, w_qkv, cos_t, sin_t, layer, seq)
        qkv3 = qkv.reshape(b, seq, -1)
        oa = _dil_call(qkv3).reshape(b * seq, -1)
        ob = _moba_call(qkv3).reshape(b * seq, -1)
        oc = _sb_call(qkv3).reshape(b * seq, -1)
        x2 = _merge_call(x2, mod, g1, oa, ob, oc, w_br_a, w_br_b, w_br_c, w_gate, b_gate3, w_out,
                         layer, seq)
        x2 = _ffn_call(x2, mod, g2, w_gu, w_down, layer, seq,
                       final_g=final_g if layer == depth - 1 else None)
    return x2.reshape(b, seq, d)
```

```python
import functools

import jax
import jax.numpy as jnp
from jax import lax
from jax.experimental import pallas as pl
from jax.experimental.pallas import tpu as pltpu

HEAD_DIM = 64
LANES = 128
DIL_GROUPS = ((128, 1), (512, 4), (2048, 16))
HEADS_PER_DIL = 4
HA = len(DIL_GROUPS) * HEADS_PER_DIL
HB = 6
HC = 6
N_HEADS = HA + HB + HC
MIX_WIDTH = N_HEADS * HEAD_DIM
BAND = 128
DIL_AHEAD = 8
BLK = 256
MOBA_TOPK = 3
KMEAN_ROWS = 16
ROPE_THETA = 10000.0
NORM_EPS = 1e-6
NEG = -1e30
LOG2E = 1.4426950408889634
SCORE_SCALE = HEAD_DIM ** -0.5 * LOG2E
VMEM_LIMIT = 56 * 1024 * 1024

NPA, NPB, NPC = HA // 2, HB // 2, HC // 2
QR0 = 0
KR0 = NPA + NPB
QC0 = 2 * (NPA + NPB)
KC0 = QC0 + NPC
V0 = KC0 + NPC
N_ROPE_COLS = 2 * (NPA + NPB) * LANES
MXU_COLS = 256


def _f32(x):
    return x.astype(jnp.float32)


def _bf16(x):
    return x.astype(jnp.bfloat16)


def _dot(a, b):
    return jnp.dot(a, b, preferred_element_type=jnp.float32)


def _dot_nt(a, b):
    return lax.dot_general(a, b, (((1,), (1,)), ((), ())), preferred_element_type=jnp.float32)


def _log2(x):
    return jnp.log(x) * LOG2E


def _iota(shape, dim):
    return lax.broadcasted_iota(jnp.int32, shape, dim)


def _modulated_norm(x, g, shift, scale):
    ms = jnp.mean(x * x, axis=-1, keepdims=True)
    y = x * lax.rsqrt(ms + NORM_EPS)
    return (y * g) * (1.0 + scale) + shift


def _qk_head_lo(shape):
    return (_iota(shape, 1) % HEAD_DIM) < (HEAD_DIM // 2)


def _split_heads(q):
    head_lo = _qk_head_lo(q.shape)
    zero = jnp.zeros_like(q)
    return jnp.where(head_lo, q, zero), jnp.where(head_lo, zero, q)


def _ada_body(c_ref, w_ref, b_ref, o_ref):
    c = c_ref[...]
    ca = c * jax.nn.sigmoid(c)
    o_ref[...] = jnp.dot(ca, w_ref[...], precision=lax.Precision.HIGHEST,
                         preferred_element_type=jnp.float32) + b_ref[...]


def _ada_call(c, w_ada, b_ada):
    depth, d, n = w_ada.shape
    b = c.shape[0]
    tn = 1536
    return pl.pallas_call(
        _ada_body,
        grid=(depth, n // tn),
        in_specs=[
            pl.BlockSpec((b, d), lambda l, j: (0, 0)),
            pl.BlockSpec((None, d, tn), lambda l, j: (l, 0, j)),
            pl.BlockSpec((None, 1, tn), lambda l, j: (l, 0, j)),
        ],
        out_specs=pl.BlockSpec((None, b, tn), lambda l, j: (l, 0, j)),
        out_shape=jax.ShapeDtypeStruct((depth, b, n), jnp.float32),
        compiler_params=pltpu.CompilerParams(
            dimension_semantics=("arbitrary", "arbitrary"), vmem_limit_bytes=VMEM_LIMIT),
        name="ada_mod",
    )(c, w_ada, b_ada.reshape(depth, 1, n))


def _arrange_qkv_weight(w_ref, wb_ref):
    n_rot_pairs = NPA + NPB
    lane = _iota((w_ref.shape[0], LANES), 1)
    keep = (lane < HEAD_DIM // 2) | (lane >= LANES - HEAD_DIM // 2)
    from_hi = lane < HEAD_DIM

    def pair(src_col, scale):
        blk = w_ref[:, src_col:src_col + LANES]
        if scale is not None:
            blk = blk * scale
        moved = jnp.where(from_hi, pltpu.roll(blk, LANES - HEAD_DIM // 2, 1), pltpu.roll(blk, HEAD_DIM // 2, 1))
        return _bf16(jnp.where(keep, blk, moved))

    dst = 0
    for part, first_pair, n_pairs in ((0, 0, n_rot_pairs), (1, 0, n_rot_pairs),
                                     (0, n_rot_pairs, NPC), (1, n_rot_pairs, NPC)):
        for p in range(first_pair, first_pair + n_pairs):
            wb_ref[:, dst:dst + LANES] = pair(part * MIX_WIDTH + p * LANES, SCORE_SCALE if part == 0 else None)
            dst += LANES
    wb_ref[:, dst:] = _bf16(w_ref[:, 2 * MIX_WIDTH:])


def _qkv_body(x_ref, mod_ref, g_ref, w_ref, cos_ref, sin_ref, o_ref, wb_ref):
    @pl.when(pl.program_id(0) == 0)
    def _():
        _arrange_qkv_weight(w_ref, wb_ref)

    h = _bf16(_modulated_norm(x_ref[...], g_ref[...], mod_ref[0:1, :], mod_ref[1:2, :]))
    cos = cos_ref[...]
    sin = sin_ref[...]
    for n in range(wb_ref.shape[1] // MXU_COLS):
        res = _dot(h, wb_ref[:, n * MXU_COLS:(n + 1) * MXU_COLS])
        for g in range(MXU_COLS // LANES):
            lo = n * MXU_COLS + g * LANES
            blk = res[:, g * LANES:(g + 1) * LANES]
            if lo < N_ROPE_COLS:
                blk = blk * cos + pltpu.roll(blk, HEAD_DIM, 1) * sin
            o_ref[:, lo:lo + LANES] = _bf16(blk)


def _qkv_call(x2, mod, norm_g, w_in, cos_t, sin_t, layer, seq):
    t, d = x2.shape
    n = w_in.shape[2]
    tm = 512
    tiles_per_seq = seq // tm
    return pl.pallas_call(
        _qkv_body,
        grid=(t // tm,),
        in_specs=[
            pl.BlockSpec((tm, d), lambda i: (i, 0)),
            pl.BlockSpec((None, None, 6, d), lambda i: (layer, i // tiles_per_seq, 0, 0)),
            pl.BlockSpec((None, 1, d), lambda i: (layer, 0, 0)),
            pl.BlockSpec((None, d, n), lambda i: (layer, 0, 0), pipeline_mode=pl.Buffered(1)),
            pl.BlockSpec((tm, LANES), lambda i: (i % tiles_per_seq, 0)),
            pl.BlockSpec((tm, LANES), lambda i: (i % tiles_per_seq, 0)),
        ],
        out_specs=pl.BlockSpec((tm, n), lambda i: (i, 0)),
        out_shape=jax.ShapeDtypeStruct((t, n), jnp.bfloat16),
        scratch_shapes=[pltpu.VMEM((d, n), jnp.bfloat16)],
        compiler_params=pltpu.CompilerParams(
            dimension_semantics=("arbitrary",), vmem_limit_bytes=VMEM_LIMIT),
        name="qkv_proj",
    )(x2, mod, norm_g, w_in, cos_t, sin_t)


def _band_scores(q, k, mask):
    return [jnp.where(mask, _dot_nt(qh, k), NEG) for qh in _split_heads(q)]


def _band_finish(scores, v, head_lo):
    outs, lses = [], []
    for s in scores:
        m = jnp.max(s, axis=-1, keepdims=True)
        e = jnp.exp2(s - m)
        den = jnp.sum(e, axis=-1, keepdims=True)
        outs.append(_dot(_bf16(e), v) / den)
        lses.append(m + _log2(den))
    return jnp.where(head_lo, outs[0], outs[1]), jnp.where(head_lo, lses[0], lses[1])


def _dil_body(q1, k1, v1, q2, k2, v2, q3, k3, v3, o_ref, f_ref, og_ref, lg_ref, *, seq):
    head_lo = _iota((BAND, LANES), 1) < HEAD_DIM
    row = _iota((BAND, 2 * BAND), 0)
    col = _iota((BAND, 2 * BAND), 1)
    mask_win = (col >= row) & (col <= row + BAND)
    mask_own = _iota((BAND, BAND), 1) <= _iota((BAND, BAND), 0)

    srcs = ((q1, k1, v1), (q2, k2, v2), (q3, k3, v3))
    for g in range(len(DIL_GROUPS)):
        for a in range(3):
            f_ref[3 * g + a] = _f32(srcs[g][a][...])

    units = []
    for g, (_, dil) in enumerate(DIL_GROUPS):
        sub_len = seq // dil
        for r in range(dil):
            qs, ks, vs = (_bf16(f_ref[3 * g + a, pl.ds(r, sub_len, stride=dil), :]) for a in range(3))
            for qb in range(sub_len // BAND):
                lo, hi = max(qb - 1, 0) * BAND, (qb + 1) * BAND
                units.append((g, r, qs[qb * BAND:hi], ks[lo:hi], vs[lo:hi], mask_own if qb == 0 else mask_win))

    scores, results = {}, {}
    for step in range(len(units) + DIL_AHEAD):
        if step < len(units):
            _, _, q, k, _, mask = units[step]
            scores[step] = _band_scores(q, k, mask)
        done = step - DIL_AHEAD
        if done >= 0:
            g, r, _, _, v, _ = units[done]
            results.setdefault((g, r), []).append(_band_finish(scores.pop(done), v, head_lo))
    for (g, r), blocks in results.items():
        dil = DIL_GROUPS[g][1]
        og_ref[g, pl.ds(r, seq // dil, stride=dil), :] = jnp.concatenate([o for o, _ in blocks], axis=0)
        lg_ref[g, pl.ds(r, seq // dil, stride=dil), :] = jnp.concatenate([l for _, l in blocks], axis=0)

    l1, l2, l3 = lg_ref[0], lg_ref[1], lg_ref[2]
    mx = jnp.maximum(jnp.maximum(l1, l2), l3)
    w1, w2, w3 = jnp.exp2(l1 - mx), jnp.exp2(l2 - mx), jnp.exp2(l3 - mx)
    o = (w1 * og_ref[0] + w2 * og_ref[1] + w3 * og_ref[2]) / (w1 + w2 + w3)
    o_ref[...] = _bf16(o)


def _dil_call(qkv3):
    b, seq, _ = qkv3.shape
    n_slot_pairs = HEADS_PER_DIL // 2

    def spec(base, g):
        return pl.BlockSpec((None, seq, LANES), lambda bi, p: (bi, 0, base + g * n_slot_pairs + p))

    in_specs = []
    for g in range(len(DIL_GROUPS)):
        in_specs += [spec(QR0, g), spec(KR0, g), spec(V0, g)]
    return pl.pallas_call(
        functools.partial(_dil_body, seq=seq),
        grid=(b, n_slot_pairs),
        in_specs=in_specs,
        out_specs=pl.BlockSpec((None, seq, LANES), lambda bi, p: (bi, 0, p)),
        out_shape=jax.ShapeDtypeStruct((b, seq, n_slot_pairs * LANES), jnp.bfloat16),
        scratch_shapes=[
            pltpu.VMEM((3 * len(DIL_GROUPS), seq, LANES), jnp.float32),
            pltpu.VMEM((len(DIL_GROUPS), seq, LANES), jnp.float32),
            pltpu.VMEM((len(DIL_GROUPS), seq, LANES), jnp.float32),
        ],
        compiler_params=pltpu.CompilerParams(
            dimension_semantics=("arbitrary", "arbitrary"), vmem_limit_bytes=VMEM_LIMIT),
        name="dilated_attn",
    )(*([qkv3] * 9))


def _moba_prepare(q_ref, k_ref, v_ref, kmean_ref, qa_ref, ka_ref, va_ref, n_blk, first):
    v = v_ref[...]
    v_lo = _iota(v.shape, 1) < HEAD_DIM
    one = jnp.ones_like(v)
    va_ref[0] = jnp.where(v_lo, v, one)
    va_ref[1] = jnp.where(v_lo, one, v)
    if first >= n_blk:
        return

    kmean_ref[...] = jnp.zeros_like(kmean_ref)
    for j in range(n_blk):
        kj = _f32(k_ref[j * BLK:(j + 1) * BLK, :])
        kmean_ref[j:j + 1, :] = jnp.mean(kj, axis=0, keepdims=True)
    km = kmean_ref[...]
    km_hi = _bf16(km)
    km_mid = _bf16(km - _f32(km_hi))
    km_lo = _bf16(km - _f32(km_hi) - _f32(km_mid))

    q = q_ref[first * BLK:, :]
    k = k_ref[...]
    nq = q.shape[0]
    q_lo = _qk_head_lo(q.shape)
    k_lo = _qk_head_lo(k.shape)
    lane = _iota(k.shape, 1)
    key_blk = _iota(k.shape, 0) // BLK
    blk_id = _iota((KMEAN_ROWS, nq), 0)
    past = blk_id < _iota((KMEAN_ROWS, nq), 1) // BLK + first
    zero = jnp.zeros_like(q)
    pad = jnp.zeros((LANES - KMEAN_ROWS, nq), jnp.float32)
    for h in range(2):
        own = q_lo if h == 0 else ~q_lo
        qh = jnp.where(own, q, zero)
        gate = _dot_nt(km_hi, qh) + _dot_nt(km_mid, qh) + _dot_nt(km_lo, qh)
        sel = jnp.zeros(gate.shape, jnp.float32)
        for j in range(n_blk - 1):
            gj = gate[j:j + 1, :]
            beats = ((gate > gj) | ((gate == gj) & (blk_id < j))) & past
            rank = jnp.sum(jnp.where(beats, 1.0, 0.0), axis=0, keepdims=True)
            sel = jnp.where((blk_id == j) & (rank < MOBA_TOPK), 1.0, sel)
        bias = jnp.where(past & (sel < 0.5), NEG, 0.0)
        bias_t = jnp.concatenate([bias, pad], axis=0).T
        base = (1 - h) * (HEAD_DIM // 2)
        if base:
            bias_t = pltpu.roll(bias_t, base, 1)
        qa_ref[h] = jnp.where(own, q, _bf16(bias_t))
        ka_ref[h] = jnp.where(k_lo if h == 0 else ~k_lo, k,
                              jnp.where(lane - base == key_blk, 1.0, 0.0).astype(k.dtype))


def _moba_scores(c, h, first, q_ref, k_ref, qa_ref, ka_ref):
    n = (c + 1) * BLK
    if c < first:
        return _dot_nt(_split_heads(q_ref[c * BLK:n, :])[h], k_ref[0:n, :])
    return _dot_nt(qa_ref[h, (c - first) * BLK:(c - first + 1) * BLK, :], ka_ref[h, 0:n, :])


def _moba_finish(c, h, s, va_ref):
    n = (c + 1) * BLK
    causal = _iota((BLK, BLK), 1) <= _iota((BLK, BLK), 0)
    own = jnp.where(causal, s[:, c * BLK:], NEG)
    m = jnp.max(own, axis=-1, keepdims=True)
    if c > 0:
        left = s[:, :c * BLK]
        m = jnp.maximum(m, jnp.max(left, axis=-1, keepdims=True))
        p = jnp.concatenate([_bf16(jnp.exp2(left - m)), _bf16(jnp.exp2(own - m))], axis=1)
    else:
        p = _bf16(jnp.exp2(own - m))
    acc = _dot(p, va_ref[h, 0:n, :])
    return acc / pltpu.roll(acc, HEAD_DIM, 1)


def _moba_body(q_ref, k_ref, v_ref, o_ref, kmean_ref, qa_ref, ka_ref, va_ref, *, n_blk, first):
    _moba_prepare(q_ref, k_ref, v_ref, kmean_ref, qa_ref, ka_ref, va_ref, n_blk, first)
    units = [(c, h) for c in range(n_blk) for h in range(2)]
    outs = {}
    pending = None
    for unit in units + [None]:
        s = None if unit is None else _moba_scores(*unit, first, q_ref, k_ref, qa_ref, ka_ref)
        if pending is not None:
            outs[pending[0]] = _moba_finish(*pending[0], pending[1], va_ref)
        pending = (unit, s)
    head_lo = _iota((BLK, LANES), 1) < HEAD_DIM
    for c in range(n_blk):
        o_ref[c * BLK:(c + 1) * BLK, :] = _bf16(jnp.where(head_lo, outs[c, 0], outs[c, 1]))


def _moba_call(qkv3):
    b, seq, _ = qkv3.shape
    n_blk = seq // BLK
    first = min(MOBA_TOPK + 1, n_blk)
    n_sel = max(seq - first * BLK, BLK)
    return pl.pallas_call(
        functools.partial(_moba_body, n_blk=n_blk, first=first),
        grid=(b, NPB),
        in_specs=[
            pl.BlockSpec((None, seq, LANES), lambda bi, p: (bi, 0, QR0 + NPA + p)),
            pl.BlockSpec((None, seq, LANES), lambda bi, p: (bi, 0, KR0 + NPA + p)),
            pl.BlockSpec((None, seq, LANES), lambda bi, p: (bi, 0, V0 + NPA + p)),
        ],
        out_specs=pl.BlockSpec((None, seq, LANES), lambda bi, p: (bi, 0, p)),
        out_shape=jax.ShapeDtypeStruct((b, seq, NPB * LANES), jnp.bfloat16),
        scratch_shapes=[pltpu.VMEM((KMEAN_ROWS, LANES), jnp.float32),
                        pltpu.VMEM((2, n_sel, LANES), jnp.bfloat16),
                        pltpu.VMEM((2, seq, LANES), jnp.bfloat16),
                        pltpu.VMEM((2, seq, LANES), jnp.bfloat16)],
        compiler_params=pltpu.CompilerParams(
            dimension_semantics=("arbitrary", "arbitrary"), vmem_limit_bytes=VMEM_LIMIT),
        name="moba_attn",
    )(qkv3, qkv3, qkv3)


def _sb_unit(c, h, q_ref, k_ref, v_ref):
    n = (c + 1) * BLK
    row = _iota((BLK, BLK), 0)
    col = _iota((BLK, BLK), 1)
    past = col < row
    from_s = jnp.where(row >= col, 1.0, 0.0).astype(jnp.bfloat16)
    from_s2 = jnp.concatenate([from_s, from_s], axis=0)

    z_all = _dot_nt(_split_heads(q_ref[c * BLK:n, :])[h], k_ref[0:n, :])
    yield None

    z_parts, tail_parts = [], []
    for j in range(c + 1):
        z = z_all[:, j * BLK:(j + 1) * BLK]
        sp = jnp.maximum(z, 0.0) + _log2(1.0 + jnp.exp2(-jnp.abs(z)))
        if j == c:
            sp = jnp.where(past, sp, 0.0)
        hi = _bf16(sp)
        lo = _bf16(sp - _f32(hi))
        tail_parts.append(_dot(jnp.concatenate([hi, lo], axis=1), from_s2))
        z_parts.append(z)
    yield None

    a_parts = [None] * (c + 1)
    beyond = None
    for j in range(c, -1, -1):
        t = z_parts[j] - tail_parts[j]
        if beyond is not None:
            t = t - beyond
        a = jnp.exp2(t)
        if j == c:
            a = jnp.where(past, a, 0.0)
        a_parts[j] = _bf16(a)
        total = tail_parts[j][:, 0:1]
        beyond = total if beyond is None else beyond + total
    yield _dot(jnp.concatenate(a_parts, axis=1), v_ref[0:n, :])


def _sb_body(q_ref, k_ref, v_ref, o_ref, *, n_blk):
    units = [(c, h) for c in range(n_blk) for h in range(2)]
    gens = [_sb_unit(c, h, q_ref, k_ref, v_ref) for c, h in units]
    n_stages = 3
    accs = {}
    for step in range(len(units) + n_stages - 1):
        for stage in range(n_stages):
            u = step - stage
            if 0 <= u < len(units):
                res = next(gens[u])
                if stage == n_stages - 1:
                    accs[units[u]] = res
    head_lo = _iota((BLK, LANES), 1) < HEAD_DIM
    for c in range(n_blk):
        o_ref[c * BLK:(c + 1) * BLK, :] = _bf16(jnp.where(head_lo, accs[c, 0], accs[c, 1]))


def _sb_call(qkv3):
    b, seq, _ = qkv3.shape
    n_blk = seq // BLK
    return pl.pallas_call(
        functools.partial(_sb_body, n_blk=n_blk),
        grid=(b, NPC),
        in_specs=[
            pl.BlockSpec((None, seq, LANES), lambda bi, p: (bi, 0, QC0 + p)),
            pl.BlockSpec((None, seq, LANES), lambda bi, p: (bi, 0, KC0 + p)),
            pl.BlockSpec((None, seq, LANES), lambda bi, p: (bi, 0, V0 + NPA + NPB + p)),
        ],
        out_specs=pl.BlockSpec((None, seq, LANES), lambda bi, p: (bi, 0, p)),
        out_shape=jax.ShapeDtypeStruct((b, seq, NPC * LANES), jnp.bfloat16),
        compiler_params=pltpu.CompilerParams(
            dimension_semantics=("arbitrary", "arbitrary"), vmem_limit_bytes=VMEM_LIMIT),
        name="stickbreak_attn",
    )(qkv3, qkv3, qkv3)


def _merge_body(x_ref, mod_ref, g_ref, oa_ref, ob_ref, oc_ref, wa32_ref, wb32_ref, wc32_ref,
                wg32_ref, bg_ref, wo32_ref, out_ref, wa_ref, wb_ref, wc_ref, wg_ref, wo_ref):
    @pl.when(pl.program_id(0) == 0)
    def _():
        for src, dst in ((wa32_ref, wa_ref), (wb32_ref, wb_ref), (wc32_ref, wc_ref),
                         (wg32_ref, wg_ref), (wo32_ref, wo_ref)):
            dst[...] = _bf16(src[...])

    x = x_ref[...]
    d = x.shape[1]
    h = _bf16(_modulated_norm(x, g_ref[...], mod_ref[0:1, :], mod_ref[1:2, :]))
    merged = None
    for br, (o_ref, w_ref) in enumerate(((oa_ref, wa_ref), (ob_ref, wb_ref), (oc_ref, wc_ref))):
        gate = jax.nn.sigmoid(_dot(h, wg_ref[:, br * d:(br + 1) * d]) + bg_ref[:, br * d:(br + 1) * d])
        term = gate * _dot(o_ref[...], w_ref[...])
        merged = term if merged is None else merged + term
    out_ref[...] = x + mod_ref[2:3, :] * _dot(_bf16(merged), wo_ref[...])


def _merge_call(x2, mod, norm_g, oa, ob, oc, w_br_a, w_br_b, w_br_c, w_gate, b_gate, w_out, layer, seq):
    t, d = x2.shape
    tm = 512
    tiles_per_seq = seq // tm

    def whole(arr):
        return pl.BlockSpec((None,) + arr.shape[1:], lambda i: (layer,) + (0,) * (arr.ndim - 1),
                            pipeline_mode=pl.Buffered(1))

    def rows(arr):
        return pl.BlockSpec((tm, arr.shape[1]), lambda i: (i, 0))

    return pl.pallas_call(
        _merge_body,
        grid=(t // tm,),
        in_specs=[
            rows(x2),
            pl.BlockSpec((None, None, 6, d), lambda i: (layer, i // tiles_per_seq, 0, 0)),
            whole(norm_g), rows(oa), rows(ob), rows(oc),
            whole(w_br_a), whole(w_br_b), whole(w_br_c), whole(w_gate), whole(b_gate), whole(w_out),
        ],
        out_specs=rows(x2),
        out_shape=jax.ShapeDtypeStruct((t, d), jnp.float32),
        scratch_shapes=[pltpu.VMEM(w.shape[1:], jnp.bfloat16)
                        for w in (w_br_a, w_br_b, w_br_c, w_gate, w_out)],
        compiler_params=pltpu.CompilerParams(
            dimension_semantics=("arbitrary",), vmem_limit_bytes=VMEM_LIMIT),
        name="gated_merge",
    )(x2, mod, norm_g, oa, ob, oc, w_br_a, w_br_b, w_br_c, w_gate, b_gate, w_out)


def _rms_norm(x, g):
    ms = jnp.mean(x * x, axis=-1, keepdims=True)
    return (x * lax.rsqrt(ms + NORM_EPS)) * g


def _ffn_body(x_ref, mod_ref, g_ref, wg_ref, wu_ref, wd_ref, *rest):
    maybe_final_g_ref, out_ref = rest[:-1], rest[-1]
    x = x_ref[...]
    h = _bf16(_modulated_norm(x, g_ref[...], mod_ref[3:4, :], mod_ref[4:5, :]))
    gt = _dot(h, wg_ref[...])
    up = _dot(h, wu_ref[...])
    act = (gt * jax.nn.sigmoid(gt)) * up
    y = x + mod_ref[5:6, :] * _dot(_bf16(act), wd_ref[...])
    for final_g_ref in maybe_final_g_ref:
        y = _rms_norm(y, final_g_ref[...])
    out_ref[...] = y


def _ffn_call(x2, mod, norm_g, w_gu, w_down, layer, seq, final_g=None):
    t, d = x2.shape
    d_ff = w_down.shape[1]
    tm = 512
    tiles_per_seq = seq // tm
    resident = pl.Buffered(1)
    in_specs = [
        pl.BlockSpec((tm, d), lambda i: (i, 0)),
        pl.BlockSpec((None, None, 6, d), lambda i: (layer, i // tiles_per_seq, 0, 0)),
        pl.BlockSpec((None, 1, d), lambda i: (layer, 0, 0)),
        pl.BlockSpec((None, d, d_ff), lambda i: (layer, 0, 0), pipeline_mode=resident),
        pl.BlockSpec((None, d, d_ff), lambda i: (layer, 0, 1), pipeline_mode=resident),
        pl.BlockSpec((None, d_ff, d), lambda i: (layer, 0, 0), pipeline_mode=resident),
    ]
    args = [x2, mod, norm_g, w_gu, w_gu, w_down]
    if final_g is not None:
        in_specs.append(pl.BlockSpec((1, d), lambda i: (0, 0)))
        args.append(final_g.reshape(1, d))
    return pl.pallas_call(
        _ffn_body,
        grid=(t // tm,),
        in_specs=in_specs,
        out_specs=pl.BlockSpec((tm, d), lambda i: (i, 0)),
        out_shape=jax.ShapeDtypeStruct((t, d), jnp.float32),
        compiler_params=pltpu.CompilerParams(
            dimension_semantics=("arbitrary",), vmem_limit_bytes=VMEM_LIMIT),
        name="swiglu_ffn",
    )(*args)


def _rope_tables(seq):
    pos = jnp.arange(seq, dtype=jnp.float32)
    inv = ROPE_THETA ** (-jnp.arange(0, HEAD_DIM, 2, dtype=jnp.float32) / HEAD_DIM)
    ang = pos[:, None] * inv[None, :]
    cos, sin = jnp.cos(ang), jnp.sin(ang)
    cos_t = jnp.tile(cos, (1, 2 * LANES // HEAD_DIM))
    sin_t = jnp.concatenate([-sin, -sin, sin, sin], axis=-1)
    return cos_t, sin_t


def kernel(x, c, w_ada, b_ada, norm1_g, w_in, w_br_a, w_br_b, w_br_c, w_gate, b_gate, w_out,
           norm2_g, w_gu, w_down, final_g):
    b, seq, d = x.shape
    depth = w_in.shape[0]
    cos_t, sin_t = _rope_tables(seq)
    w_gu, w_down = _bf16(w_gu), _bf16(w_down)
    b_gate3 = b_gate.reshape(depth, 1, -1)
    g1 = norm1_g.reshape(depth, 1, d)
    g2 = norm2_g.reshape(depth, 1, d)

    mod = _ada_call(c, w_ada, b_ada).reshape(depth, b, 6, d)
    x2 = x.reshape(b * seq, d)
    for layer in range(depth):
        qkv = _qkv_call(x2, mod, g1, w_in, cos_t, sin_t, layer, seq)
        qkv3 = qkv.reshape(b, seq, -1)
        oa = _dil_call(qkv3).reshape(b * seq, -1)
        ob = _moba_call(qkv3).reshape(b * seq, -1)
        oc = _sb_call(qkv3).reshape(b * seq, -1)
        x2 = _merge_call(x2, mod, g1, oa, ob, oc, w_br_a, w_br_b, w_br_c, w_gate, b_gate3, w_out,
                         layer, seq)
        x2 = _ffn_call(x2, mod, g2, w_gu, w_down, layer, seq,
                       final_g=final_g if layer == depth - 1 else None)
    return x2.reshape(b, seq, d)
```

```python
import functools

import jax
import jax.numpy as jnp
from jax import lax
from jax.experimental import pallas as pl
from jax.experimental.pallas import tpu as pltpu

HEAD_DIM = 64
LANES = 128
DIL_GROUPS = ((128, 1), (512, 4), (2048, 16))
HEADS_PER_DIL = 4
HA = len(DIL_GROUPS) * HEADS_PER_DIL
HB = 6
HC = 6
N_HEADS = HA + HB + HC
MIX_WIDTH = N_HEADS * HEAD_DIM
BAND = 128
DIL_AHEAD = 4
BLK = 256
MOBA_TOPK = 3
KMEAN_ROWS = 16
ROPE_THETA = 10000.0
NORM_EPS = 1e-6
NEG = -1e30
LOG2E = 1.4426950408889634
SCORE_SCALE = HEAD_DIM ** -0.5 * LOG2E
VMEM_LIMIT = 56 * 1024 * 1024

NPA, NPB, NPC = HA // 2, HB // 2, HC // 2
QR0 = 0
KR0 = NPA + NPB
QC0 = 2 * (NPA + NPB)
KC0 = QC0 + NPC
V0 = KC0 + NPC
N_ROPE_COLS = 2 * (NPA + NPB) * LANES
MXU_COLS = 256


def _f32(x):
    return x.astype(jnp.float32)


def _bf16(x):
    return x.astype(jnp.bfloat16)


def _dot(a, b):
    return jnp.dot(a, b, preferred_element_type=jnp.float32)


def _dot_nt(a, b):
    return lax.dot_general(a, b, (((1,), (1,)), ((), ())), preferred_element_type=jnp.float32)


def _log2(x):
    return jnp.log(x) * LOG2E


def _iota(shape, dim):
    return lax.broadcasted_iota(jnp.int32, shape, dim)


def _modulated_norm(x, g, shift, scale):
    ms = jnp.mean(x * x, axis=-1, keepdims=True)
    y = x * lax.rsqrt(ms + NORM_EPS)
    return (y * g) * (1.0 + scale) + shift


def _qk_head_lo(shape):
    return (_iota(shape, 1) % HEAD_DIM) < (HEAD_DIM // 2)


def _split_heads(q):
    head_lo = _qk_head_lo(q.shape)
    zero = jnp.zeros_like(q)
    return jnp.where(head_lo, q, zero), jnp.where(head_lo, zero, q)


def _ada_body(c_ref, w_ref, b_ref, o_ref):
    c = c_ref[...]
    ca = c * jax.nn.sigmoid(c)
    o_ref[...] = jnp.dot(ca, w_ref[...], precision=lax.Precision.HIGHEST,
                         preferred_element_type=jnp.float32) + b_ref[...]


def _ada_call(c, w_ada, b_ada):
    depth, d, n = w_ada.shape
    b = c.shape[0]
    tn = 1536
    return pl.pallas_call(
        _ada_body,
        grid=(depth, n // tn),
        in_specs=[
            pl.BlockSpec((b, d), lambda l, j: (0, 0)),
            pl.BlockSpec((None, d, tn), lambda l, j: (l, 0, j)),
            pl.BlockSpec((None, 1, tn), lambda l, j: (l, 0, j)),
        ],
        out_specs=pl.BlockSpec((None, b, tn), lambda l, j: (l, 0, j)),
        out_shape=jax.ShapeDtypeStruct((depth, b, n), jnp.float32),
        compiler_params=pltpu.CompilerParams(
            dimension_semantics=("arbitrary", "arbitrary"), vmem_limit_bytes=VMEM_LIMIT),
        name="ada_mod",
    )(c, w_ada, b_ada.reshape(depth, 1, n))


def _arrange_qkv_weight(w_ref, wb_ref):
    n_rot_pairs = NPA + NPB
    lane = _iota((w_ref.shape[0], LANES), 1)
    keep = (lane < HEAD_DIM // 2) | (lane >= LANES - HEAD_DIM // 2)
    from_hi = lane < HEAD_DIM

    def pair(src_col, scale):
        blk = w_ref[:, src_col:src_col + LANES]
        if scale is not None:
            blk = blk * scale
        moved = jnp.where(from_hi, pltpu.roll(blk, LANES - HEAD_DIM // 2, 1), pltpu.roll(blk, HEAD_DIM // 2, 1))
        return _bf16(jnp.where(keep, blk, moved))

    dst = 0
    for part, first_pair, n_pairs in ((0, 0, n_rot_pairs), (1, 0, n_rot_pairs),
                                     (0, n_rot_pairs, NPC), (1, n_rot_pairs, NPC)):
        for p in range(first_pair, first_pair + n_pairs):
            wb_ref[:, dst:dst + LANES] = pair(part * MIX_WIDTH + p * LANES, SCORE_SCALE if part == 0 else None)
            dst += LANES
    wb_ref[:, dst:] = _bf16(w_ref[:, 2 * MIX_WIDTH:])


def _qkv_body(x_ref, mod_ref, g_ref, w_ref, cos_ref, sin_ref, o_ref, wb_ref):
    @pl.when(pl.program_id(0) == 0)
    def _():
        _arrange_qkv_weight(w_ref, wb_ref)

    h = _bf16(_modulated_norm(x_ref[...], g_ref[...], mod_ref[0:1, :], mod_ref[1:2, :]))
    cos = cos_ref[...]
    sin = sin_ref[...]
    for n in range(wb_ref.shape[1] // MXU_COLS):
        res = _dot(h, wb_ref[:, n * MXU_COLS:(n + 1) * MXU_COLS])
        for g in range(MXU_COLS // LANES):
            lo = n * MXU_COLS + g * LANES
            blk = res[:, g * LANES:(g + 1) * LANES]
            if lo < N_ROPE_COLS:
                blk = blk * cos + pltpu.roll(blk, HEAD_DIM, 1) * sin
            o_ref[:, lo:lo + LANES] = _bf16(blk)


def _qkv_call(x2, mod, norm_g, w_in, cos_t, sin_t, layer, seq):
    t, d = x2.shape
    n = w_in.shape[2]
    tm = 512
    tiles_per_seq = seq // tm
    return pl.pallas_call(
        _qkv_body,
        grid=(t // tm,),
        in_specs=[
            pl.BlockSpec((tm, d), lambda i: (i, 0)),
            pl.BlockSpec((None, None, 6, d), lambda i: (layer, i // tiles_per_seq, 0, 0)),
            pl.BlockSpec((None, 1, d), lambda i: (layer, 0, 0)),
            pl.BlockSpec((None, d, n), lambda i: (layer, 0, 0), pipeline_mode=pl.Buffered(1)),
            pl.BlockSpec((tm, LANES), lambda i: (i % tiles_per_seq, 0)),
            pl.BlockSpec((tm, LANES), lambda i: (i % tiles_per_seq, 0)),
        ],
        out_specs=pl.BlockSpec((tm, n), lambda i: (i, 0)),
        out_shape=jax.ShapeDtypeStruct((t, n), jnp.bfloat16),
        scratch_shapes=[pltpu.VMEM((d, n), jnp.bfloat16)],
        compiler_params=pltpu.CompilerParams(
            dimension_semantics=("arbitrary",), vmem_limit_bytes=VMEM_LIMIT),
        name="qkv_proj",
    )(x2, mod, norm_g, w_in, cos_t, sin_t)


def _band_scores(q, k, mask):
    return [jnp.where(mask, _dot_nt(qh, k), NEG) for qh in _split_heads(q)]


def _band_finish(scores, v, head_lo):
    outs, lses = [], []
    for s in scores:
        m = jnp.max(s, axis=-1, keepdims=True)
        e = jnp.exp2(s - m)
        den = jnp.sum(e, axis=-1, keepdims=True)
        outs.append(_dot(_bf16(e), v) / den)
        lses.append(m + _log2(den))
    return jnp.where(head_lo, outs[0], outs[1]), jnp.where(head_lo, lses[0], lses[1])


def _dil_body(q1, k1, v1, q2, k2, v2, q3, k3, v3, o_ref, f_ref, og_ref, lg_ref, *, seq):
    head_lo = _iota((BAND, LANES), 1) < HEAD_DIM
    row = _iota((BAND, 2 * BAND), 0)
    col = _iota((BAND, 2 * BAND), 1)
    mask_win = (col >= row) & (col <= row + BAND)
    mask_own = _iota((BAND, BAND), 1) <= _iota((BAND, BAND), 0)

    srcs = ((q1, k1, v1), (q2, k2, v2), (q3, k3, v3))
    slot = {}
    for g, (_, dil) in enumerate(DIL_GROUPS):
        if dil > 1:
            for a in range(3):
                slot[g, a] = len(slot)
                f_ref[slot[g, a]] = _f32(srcs[g][a][...])

    units = []
    for g, (_, dil) in enumerate(DIL_GROUPS):
        sub_len = seq // dil
        for r in range(dil):
            if dil > 1:
                qs, ks, vs = (_bf16(f_ref[slot[g, a], pl.ds(r, sub_len, stride=dil), :]) for a in range(3))
            else:
                qs, ks, vs = (srcs[g][a][...] for a in range(3))
            for qb in range(sub_len // BAND):
                lo, hi = max(qb - 1, 0) * BAND, (qb + 1) * BAND
                units.append((g, r, qs[qb * BAND:hi], ks[lo:hi], vs[lo:hi], mask_own if qb == 0 else mask_win))

    scores, results = {}, {}
    for step in range(len(units) + DIL_AHEAD):
        if step < len(units):
            _, _, q, k, _, mask = units[step]
            scores[step] = _band_scores(q, k, mask)
        done = step - DIL_AHEAD
        if done >= 0:
            g, r, _, _, v, _ = units[done]
            results.setdefault((g, r), []).append(_band_finish(scores.pop(done), v, head_lo))
    for (g, r), blocks in results.items():
        dil = DIL_GROUPS[g][1]
        og_ref[g, pl.ds(r, seq // dil, stride=dil), :] = jnp.concatenate([o for o, _ in blocks], axis=0)
        lg_ref[g, pl.ds(r, seq // dil, stride=dil), :] = jnp.concatenate([l for _, l in blocks], axis=0)

    l1, l2, l3 = lg_ref[0], lg_ref[1], lg_ref[2]
    mx = jnp.maximum(jnp.maximum(l1, l2), l3)
    w1, w2, w3 = jnp.exp2(l1 - mx), jnp.exp2(l2 - mx), jnp.exp2(l3 - mx)
    o = (w1 * og_ref[0] + w2 * og_ref[1] + w3 * og_ref[2]) / (w1 + w2 + w3)
    o_ref[...] = _bf16(o)


def _dil_call(qkv3):
    b, seq, _ = qkv3.shape
    n_slot_pairs = HEADS_PER_DIL // 2

    def spec(base, g):
        return pl.BlockSpec((None, seq, LANES), lambda bi, p: (bi, 0, base + g * n_slot_pairs + p))

    in_specs = []
    for g in range(len(DIL_GROUPS)):
        in_specs += [spec(QR0, g), spec(KR0, g), spec(V0, g)]
    return pl.pallas_call(
        functools.partial(_dil_body, seq=seq),
        grid=(b, n_slot_pairs),
        in_specs=in_specs,
        out_specs=pl.BlockSpec((None, seq, LANES), lambda bi, p: (bi, 0, p)),
        out_shape=jax.ShapeDtypeStruct((b, seq, n_slot_pairs * LANES), jnp.bfloat16),
        scratch_shapes=[
            pltpu.VMEM((3 * sum(dil > 1 for _, dil in DIL_GROUPS), seq, LANES), jnp.float32),
            pltpu.VMEM((len(DIL_GROUPS), seq, LANES), jnp.float32),
            pltpu.VMEM((len(DIL_GROUPS), seq, LANES), jnp.float32),
        ],
        compiler_params=pltpu.CompilerParams(
            dimension_semantics=("arbitrary", "arbitrary"), vmem_limit_bytes=VMEM_LIMIT),
        name="dilated_attn",
    )(*([qkv3] * 9))


def _moba_prepare(q_ref, k_ref, v_ref, kmean_ref, qa_ref, ka_ref, va_ref, n_blk, first):
    v = v_ref[...]
    v_lo = _iota(v.shape, 1) < HEAD_DIM
    one = jnp.ones_like(v)
    va_ref[0] = jnp.where(v_lo, v, one)
    va_ref[1] = jnp.where(v_lo, one, v)
    if first >= n_blk:
        return

    kmean_ref[...] = jnp.zeros_like(kmean_ref)
    for j in range(n_blk):
        kj = _f32(k_ref[j * BLK:(j + 1) * BLK, :])
        kmean_ref[j:j + 1, :] = jnp.mean(kj, axis=0, keepdims=True)
    km = kmean_ref[...]
    km_hi = _bf16(km)
    km_mid = _bf16(km - _f32(km_hi))
    km_lo = _bf16(km - _f32(km_hi) - _f32(km_mid))

    q = q_ref[first * BLK:, :]
    k = k_ref[...]
    nq = q.shape[0]
    q_lo = _qk_head_lo(q.shape)
    k_lo = _qk_head_lo(k.shape)
    lane = _iota(k.shape, 1)
    key_blk = _iota(k.shape, 0) // BLK
    blk_id = _iota((KMEAN_ROWS, nq), 0)
    past = blk_id < _iota((KMEAN_ROWS, nq), 1) // BLK + first
    zero = jnp.zeros_like(q)
    pad = jnp.zeros((LANES - KMEAN_ROWS, nq), jnp.float32)
    for h in range(2):
        own = q_lo if h == 0 else ~q_lo
        qh = jnp.where(own, q, zero)
        gate = _dot_nt(km_hi, qh) + _dot_nt(km_mid, qh) + _dot_nt(km_lo, qh)
        sel = jnp.zeros(gate.shape, jnp.float32)
        for j in range(n_blk - 1):
            gj = gate[j:j + 1, :]
            beats = ((gate > gj) | ((gate == gj) & (blk_id < j))) & past
            rank = jnp.sum(jnp.where(beats, 1.0, 0.0), axis=0, keepdims=True)
            sel = jnp.where((blk_id == j) & (rank < MOBA_TOPK), 1.0, sel)
        bias = jnp.where(past & (sel < 0.5), NEG, 0.0)
        bias_t = jnp.concatenate([bias, pad], axis=0).T
        base = (1 - h) * (HEAD_DIM // 2)
        if base:
            bias_t = pltpu.roll(bias_t, base, 1)
        qa_ref[h] = jnp.where(own, q, _bf16(bias_t))
        ka_ref[h] = jnp.where(k_lo if h == 0 else ~k_lo, k,
                              jnp.where(lane - base == key_blk, 1.0, 0.0).astype(k.dtype))


def _moba_scores(c, h, first, q_ref, k_ref, qa_ref, ka_ref):
    n = (c + 1) * BLK
    if c < first:
        return _dot_nt(_split_heads(q_ref[c * BLK:n, :])[h], k_ref[0:n, :])
    return _dot_nt(qa_ref[h, (c - first) * BLK:(c - first + 1) * BLK, :], ka_ref[h, 0:n, :])


def _moba_finish(c, h, s, va_ref):
    n = (c + 1) * BLK
    causal = _iota((BLK, BLK), 1) <= _iota((BLK, BLK), 0)
    own = jnp.where(causal, s[:, c * BLK:], NEG)
    m = jnp.max(own, axis=-1, keepdims=True)
    if c > 0:
        left = s[:, :c * BLK]
        m = jnp.maximum(m, jnp.max(left, axis=-1, keepdims=True))
        p = jnp.concatenate([_bf16(jnp.exp2(left - m)), _bf16(jnp.exp2(own - m))], axis=1)
    else:
        p = _bf16(jnp.exp2(own - m))
    acc = _dot(p, va_ref[h, 0:n, :])
    return acc / pltpu.roll(acc, HEAD_DIM, 1)


def _moba_body(q_ref, k_ref, v_ref, o_ref, kmean_ref, qa_ref, ka_ref, va_ref, *, n_blk, first):
    _moba_prepare(q_ref, k_ref, v_ref, kmean_ref, qa_ref, ka_ref, va_ref, n_blk, first)
    units = [(c, h) for c in range(n_blk) for h in range(2)]
    outs = {}
    pending = None
    for unit in units + [None]:
        s = None if unit is None else _moba_scores(*unit, first, q_ref, k_ref, qa_ref, ka_ref)
        if pending is not None:
            outs[pending[0]] = _moba_finish(*pending[0], pending[1], va_ref)
        pending = (unit, s)
    head_lo = _iota((BLK, LANES), 1) < HEAD_DIM
    for c in range(n_blk):
        o_ref[c * BLK:(c + 1) * BLK, :] = _bf16(jnp.where(head_lo, outs[c, 0], outs[c, 1]))


def _moba_call(qkv3):
    b, seq, _ = qkv3.shape
    n_blk = seq // BLK
    first = min(MOBA_TOPK + 1, n_blk)
    n_sel = max(seq - first * BLK, BLK)
    return pl.pallas_call(
        functools.partial(_moba_body, n_blk=n_blk, first=first),
        grid=(b, NPB),
        in_specs=[
            pl.BlockSpec((None, seq, LANES), lambda bi, p: (bi, 0, QR0 + NPA + p)),
            pl.BlockSpec((None, seq, LANES), lambda bi, p: (bi, 0, KR0 + NPA + p)),
            pl.BlockSpec((None, seq, LANES), lambda bi, p: (bi, 0, V0 + NPA + p)),
        ],
        out_specs=pl.BlockSpec((None, seq, LANES), lambda bi, p: (bi, 0, p)),
        out_shape=jax.ShapeDtypeStruct((b, seq, NPB * LANES), jnp.bfloat16),
        scratch_shapes=[pltpu.VMEM((KMEAN_ROWS, LANES), jnp.float32),
                        pltpu.VMEM((2, n_sel, LANES), jnp.bfloat16),
                        pltpu.VMEM((2, seq, LANES), jnp.bfloat16),
                        pltpu.VMEM((2, seq, LANES), jnp.bfloat16)],
        compiler_params=pltpu.CompilerParams(
            dimension_semantics=("arbitrary", "arbitrary"), vmem_limit_bytes=VMEM_LIMIT),
        name="moba_attn",
    )(qkv3, qkv3, qkv3)


def _sb_unit(c, h, q_ref, k_ref, v_ref):
    n = (c + 1) * BLK
    row = _iota((BLK, BLK), 0)
    col = _iota((BLK, BLK), 1)
    past = col < row
    from_s = jnp.where(row >= col, 1.0, 0.0).astype(jnp.bfloat16)
    from_s2 = jnp.concatenate([from_s, from_s], axis=0)

    z_all = _dot_nt(_split_heads(q_ref[c * BLK:n, :])[h], k_ref[0:n, :])
    yield None

    z_parts, tail_parts = [], []
    for j in range(c + 1):
        z = z_all[:, j * BLK:(j + 1) * BLK]
        sp = jnp.maximum(z, 0.0) + _log2(1.0 + jnp.exp2(-jnp.abs(z)))
        if j == c:
            sp = jnp.where(past, sp, 0.0)
        hi = _bf16(sp)
        lo = _bf16(sp - _f32(hi))
        tail_parts.append(_dot(jnp.concatenate([hi, lo], axis=1), from_s2))
        z_parts.append(z)
    yield None

    a_parts = [None] * (c + 1)
    beyond = None
    for j in range(c, -1, -1):
        t = z_parts[j] - tail_parts[j]
        if beyond is not None:
            t = t - beyond
        a = jnp.exp2(t)
        if j == c:
            a = jnp.where(past, a, 0.0)
        a_parts[j] = _bf16(a)
        total = tail_parts[j][:, 0:1]
        beyond = total if beyond is None else beyond + total
    yield _dot(jnp.concatenate(a_parts, axis=1), v_ref[0:n, :])


def _sb_body(q_ref, k_ref, v_ref, o_ref, *, n_blk):
    units = [(c, h) for c in range(n_blk) for h in range(2)]
    gens = [_sb_unit(c, h, q_ref, k_ref, v_ref) for c, h in units]
    n_stages = 3
    accs = {}
    for step in range(len(units) + n_stages - 1):
        for stage in range(n_stages):
            u = step - stage
            if 0 <= u < len(units):
                res = next(gens[u])
                if stage == n_stages - 1:
                    accs[units[u]] = res
    head_lo = _iota((BLK, LANES), 1) < HEAD_DIM
    for c in range(n_blk):
        o_ref[c * BLK:(c + 1) * BLK, :] = _bf16(jnp.where(head_lo, accs[c, 0], accs[c, 1]))


def _sb_call(qkv3):
    b, seq, _ = qkv3.shape
    n_blk = seq // BLK
    return pl.pallas_call(
        functools.partial(_sb_body, n_blk=n_blk),
        grid=(b, NPC),
        in_specs=[
            pl.BlockSpec((None, seq, LANES), lambda bi, p: (bi, 0, QC0 + p)),
            pl.BlockSpec((None, seq, LANES), lambda bi, p: (bi, 0, KC0 + p)),
            pl.BlockSpec((None, seq, LANES), lambda bi, p: (bi, 0, V0 + NPA + NPB + p)),
        ],
        out_specs=pl.BlockSpec((None, seq, LANES), lambda bi, p: (bi, 0, p)),
        out_shape=jax.ShapeDtypeStruct((b, seq, NPC * LANES), jnp.bfloat16),
        compiler_params=pltpu.CompilerParams(
            dimension_semantics=("arbitrary", "arbitrary"), vmem_limit_bytes=VMEM_LIMIT),
        name="stickbreak_attn",
    )(qkv3, qkv3, qkv3)


def _merge_body(x_ref, mod_ref, g_ref, oa_ref, ob_ref, oc_ref, wa32_ref, wb32_ref, wc32_ref,
                wg32_ref, bg_ref, wo32_ref, out_ref, wa_ref, wb_ref, wc_ref, wg_ref, wo_ref):
    @pl.when(pl.program_id(0) == 0)
    def _():
        for src, dst in ((wa32_ref, wa_ref), (wb32_ref, wb_ref), (wc32_ref, wc_ref),
                         (wg32_ref, wg_ref), (wo32_ref, wo_ref)):
            dst[...] = _bf16(src[...])

    x = x_ref[...]
    d = x.shape[1]
    h = _bf16(_modulated_norm(x, g_ref[...], mod_ref[0:1, :], mod_ref[1:2, :]))
    merged = None
    for br, (o_ref, w_ref) in enumerate(((oa_ref, wa_ref), (ob_ref, wb_ref), (oc_ref, wc_ref))):
        gate = jax.nn.sigmoid(_dot(h, wg_ref[:, br * d:(br + 1) * d]) + bg_ref[:, br * d:(br + 1) * d])
        term = gate * _dot(o_ref[...], w_ref[...])
        merged = term if merged is None else merged + term
    out_ref[...] = x + mod_ref[2:3, :] * _dot(_bf16(merged), wo_ref[...])


def _merge_call(x2, mod, norm_g, oa, ob, oc, w_br_a, w_br_b, w_br_c, w_gate, b_gate, w_out, layer, seq):
    t, d = x2.shape
    tm = 512
    tiles_per_seq = seq // tm

    def whole(arr):
        return pl.BlockSpec((None,) + arr.shape[1:], lambda i: (layer,) + (0,) * (arr.ndim - 1),
                            pipeline_mode=pl.Buffered(1))

    def rows(arr):
        return pl.BlockSpec((tm, arr.shape[1]), lambda i: (i, 0))

    return pl.pallas_call(
        _merge_body,
        grid=(t // tm,),
        in_specs=[
            rows(x2),
            pl.BlockSpec((None, None, 6, d), lambda i: (layer, i // tiles_per_seq, 0, 0)),
            whole(norm_g), rows(oa), rows(ob), rows(oc),
            whole(w_br_a), whole(w_br_b), whole(w_br_c), whole(w_gate), whole(b_gate), whole(w_out),
        ],
        out_specs=rows(x2),
        out_shape=jax.ShapeDtypeStruct((t, d), jnp.float32),
        scratch_shapes=[pltpu.VMEM(w.shape[1:], jnp.bfloat16)
                        for w in (w_br_a, w_br_b, w_br_c, w_gate, w_out)],
        compiler_params=pltpu.CompilerParams(
            dimension_semantics=("arbitrary",), vmem_limit_bytes=VMEM_LIMIT),
        name="gated_merge",
    )(x2, mod, norm_g, oa, ob, oc, w_br_a, w_br_b, w_br_c, w_gate, b_gate, w_out)


def _rms_norm(x, g):
    ms = jnp.mean(x * x, axis=-1, keepdims=True)
    return (x * lax.rsqrt(ms + NORM_EPS)) * g


def _ffn_body(x_ref, mod_ref, g_ref, wg_ref, wu_ref, wd_ref, *rest):
    maybe_final_g_ref, out_ref = rest[:-1], rest[-1]
    x = x_ref[...]
    h = _bf16(_modulated_norm(x, g_ref[...], mod_ref[3:4, :], mod_ref[4:5, :]))
    gt = _dot(h, wg_ref[...])
    up = _dot(h, wu_ref[...])
    act = (gt * jax.nn.sigmoid(gt)) * up
    y = x + mod_ref[5:6, :] * _dot(_bf16(act), wd_ref[...])
    for final_g_ref in maybe_final_g_ref:
        y = _rms_norm(y, final_g_ref[...])
    out_ref[...] = y


def _ffn_call(x2, mod, norm_g, w_gu, w_down, layer, seq, final_g=None):
    t, d = x2.shape
    d_ff = w_down.shape[1]
    tm = 512
    tiles_per_seq = seq // tm
    resident = pl.Buffered(1)
    in_specs = [
        pl.BlockSpec((tm, d), lambda i: (i, 0)),
        pl.BlockSpec((None, None, 6, d), lambda i: (layer, i // tiles_per_seq, 0, 0)),
        pl.BlockSpec((None, 1, d), lambda i: (layer, 0, 0)),
        pl.BlockSpec((None, d, d_ff), lambda i: (layer, 0, 0), pipeline_mode=resident),
        pl.BlockSpec((None, d, d_ff), lambda i: (layer, 0, 1), pipeline_mode=resident),
        pl.BlockSpec((None, d_ff, d), lambda i: (layer, 0, 0), pipeline_mode=resident),
    ]
    args = [x2, mod, norm_g, w_gu, w_gu, w_down]
    if final_g is not None:
        in_specs.append(pl.BlockSpec((1, d), lambda i: (0, 0)))
        args.append(final_g.reshape(1, d))
    return pl.pallas_call(
        _ffn_body,
        grid=(t // tm,),
        in_specs=in_specs,
        out_specs=pl.BlockSpec((tm, d), lambda i: (i, 0)),
        out_shape=jax.ShapeDtypeStruct((t, d), jnp.float32),
        compiler_params=pltpu.CompilerParams(
            dimension_semantics=("arbitrary",), vmem_limit_bytes=VMEM_LIMIT),
        name="swiglu_ffn",
    )(*args)


def _rope_tables(seq):
    pos = jnp.arange(seq, dtype=jnp.float32)
    inv = ROPE_THETA ** (-jnp.arange(0, HEAD_DIM, 2, dtype=jnp.float32) / HEAD_DIM)
    ang = pos[:, None] * inv[None, :]
    cos, sin = jnp.cos(ang), jnp.sin(ang)
    cos_t = jnp.tile(cos, (1, 2 * LANES // HEAD_DIM))
    sin_t = jnp.concatenate([-sin, -sin, sin, sin], axis=-1)
    return cos_t, sin_t


def kernel(x, c, w_ada, b_ada, norm1_g, w_in, w_br_a, w_br_b, w_br_c, w_gate, b_gate, w_out,
           norm2_g, w_gu, w_down, final_g):
    b, seq, d = x.shape
    depth = w_in.shape[0]
    cos_t, sin_t = _rope_tables(seq)
    w_gu, w_down = _bf16(w_gu), _bf16(w_down)
    b_gate3 = b_gate.reshape(depth, 1, -1)
    g1 = norm1_g.reshape(depth, 1, d)
    g2 = norm2_g.reshape(depth, 1, d)

    mod = _ada_call(c, w_ada, b_ada).reshape(depth, b, 6, d)
    x2 = x.reshape(b * seq, d)
    for layer in range(depth):
        qkv = _qkv_call(x2, mod, g1, w_in, cos_t, sin_t, layer, seq)
        qkv3 = qkv.reshape(b, seq, -1)
        oa = _dil_call(qkv3).reshape(b * seq, -1)
        ob = _moba_call(qkv3).reshape(b * seq, -1)
        oc = _sb_call(qkv3).reshape(b * seq, -1)
        x2 = _merge_call(x2, mod, g1, oa, ob, oc, w_br_a, w_br_b, w_br_c, w_gate, b_gate3, w_out,
                         layer, seq)
        x2 = _ffn_call(x2, mod, g2, w_gu, w_down, layer, seq,
                       final_g=final_g if layer == depth - 1 else None)
    return x2.reshape(b, seq, d)
```

```python
import functools

import jax
import jax.numpy as jnp
from jax import lax
from jax.experimental import pallas as pl
from jax.experimental.pallas import tpu as pltpu

HEAD_DIM = 64
LANES = 128
DIL_GROUPS = ((128, 1), (512, 4), (2048, 16))
HEADS_PER_DIL = 4
HA = len(DIL_GROUPS) * HEADS_PER_DIL
HB = 6
HC = 6
N_HEADS = HA + HB + HC
MIX_WIDTH = N_HEADS * HEAD_DIM
BAND = 128
DIL_AHEAD = 4
BLK = 256
MOBA_TOPK = 3
KMEAN_ROWS = 16
MOBA_AHEAD = 2
ROPE_THETA = 10000.0
NORM_EPS = 1e-6
NEG = -1e30
LOG2E = 1.4426950408889634
SCORE_SCALE = HEAD_DIM ** -0.5 * LOG2E
VMEM_LIMIT = 56 * 1024 * 1024

NPA, NPB, NPC = HA // 2, HB // 2, HC // 2
QR0 = 0
KR0 = NPA + NPB
QC0 = 2 * (NPA + NPB)
KC0 = QC0 + NPC
V0 = KC0 + NPC
N_ROPE_COLS = 2 * (NPA + NPB) * LANES
MXU_COLS = 256
ROW_TILE = 512
ADA_COL_TILE = 1536


def _f32(x):
    return x.astype(jnp.float32)


def _bf16(x):
    return x.astype(jnp.bfloat16)


def _dot(a, b):
    return jnp.dot(a, b, preferred_element_type=jnp.float32)


def _dot_nt(a, b):
    return lax.dot_general(a, b, (((1,), (1,)), ((), ())), preferred_element_type=jnp.float32)


def _log2(x):
    return jnp.log(x) * LOG2E


def _iota(shape, dim):
    return lax.broadcasted_iota(jnp.int32, shape, dim)


def _rms_norm(x, g):
    ms = jnp.mean(x * x, axis=-1, keepdims=True)
    return (x * lax.rsqrt(ms + NORM_EPS)) * g


def _modulated_norm(x, g, shift, scale):
    return _rms_norm(x, g) * (1.0 + scale) + shift


def _qk_head_lo(shape):
    return (_iota(shape, 1) % HEAD_DIM) < (HEAD_DIM // 2)


def _split_heads(q):
    head_lo = _qk_head_lo(q.shape)
    zero = jnp.zeros_like(q)
    return jnp.where(head_lo, q, zero), jnp.where(head_lo, zero, q)


def _ada_body(c_ref, w_ref, b_ref, o_ref):
    c = c_ref[...]
    ca = c * jax.nn.sigmoid(c)
    o_ref[...] = jnp.dot(ca, w_ref[...], precision=lax.Precision.HIGHEST,
                         preferred_element_type=jnp.float32) + b_ref[...]


def _ada_call(c, w_ada, b_ada):
    depth, d, n = w_ada.shape
    b = c.shape[0]
    tn = ADA_COL_TILE
    return pl.pallas_call(
        _ada_body,
        grid=(depth, n // tn),
        in_specs=[
            pl.BlockSpec((b, d), lambda l, j: (0, 0)),
            pl.BlockSpec((None, d, tn), lambda l, j: (l, 0, j)),
            pl.BlockSpec((None, 1, tn), lambda l, j: (l, 0, j)),
        ],
        out_specs=pl.BlockSpec((None, b, tn), lambda l, j: (l, 0, j)),
        out_shape=jax.ShapeDtypeStruct((depth, b, n), jnp.float32),
        compiler_params=pltpu.CompilerParams(
            dimension_semantics=("arbitrary", "arbitrary"), vmem_limit_bytes=VMEM_LIMIT),
        name="ada_mod",
    )(c, w_ada, b_ada.reshape(depth, 1, n))


def _arrange_qkv_weight(w_ref, wb_ref):
    n_rot_pairs = NPA + NPB
    lane = _iota((w_ref.shape[0], LANES), 1)
    keep = (lane < HEAD_DIM // 2) | (lane >= LANES - HEAD_DIM // 2)
    from_hi = lane < HEAD_DIM

    def pair(src_col, scale):
        blk = w_ref[:, src_col:src_col + LANES]
        if scale is not None:
            blk = blk * scale
        moved = jnp.where(from_hi, pltpu.roll(blk, LANES - HEAD_DIM // 2, 1), pltpu.roll(blk, HEAD_DIM // 2, 1))
        return _bf16(jnp.where(keep, blk, moved))

    dst = 0
    for part, first_pair, n_pairs in ((0, 0, n_rot_pairs), (1, 0, n_rot_pairs),
                                     (0, n_rot_pairs, NPC), (1, n_rot_pairs, NPC)):
        for p in range(first_pair, first_pair + n_pairs):
            wb_ref[:, dst:dst + LANES] = pair(part * MIX_WIDTH + p * LANES, SCORE_SCALE if part == 0 else None)
            dst += LANES
    wb_ref[:, dst:] = _bf16(w_ref[:, 2 * MIX_WIDTH:])


def _qkv_body(x_ref, mod_ref, g_ref, w_ref, cos_ref, sin_ref, o_ref, wb_ref):
    @pl.when(pl.program_id(0) == 0)
    def _():
        _arrange_qkv_weight(w_ref, wb_ref)

    h = _bf16(_modulated_norm(x_ref[...], g_ref[...], mod_ref[0:1, :], mod_ref[1:2, :]))
    cos = cos_ref[...]
    sin = sin_ref[...]
    for n in range(wb_ref.shape[1] // MXU_COLS):
        res = _dot(h, wb_ref[:, n * MXU_COLS:(n + 1) * MXU_COLS])
        for g in range(MXU_COLS // LANES):
            lo = n * MXU_COLS + g * LANES
            blk = res[:, g * LANES:(g + 1) * LANES]
            if lo < N_ROPE_COLS:
                blk = blk * cos + pltpu.roll(blk, HEAD_DIM, 1) * sin
            o_ref[:, lo:lo + LANES] = _bf16(blk)


def _qkv_call(x2, mod, norm_g, w_in, cos_t, sin_t, layer, seq):
    t, d = x2.shape
    n = w_in.shape[2]
    tm = ROW_TILE
    tiles_per_seq = seq // tm
    return pl.pallas_call(
        _qkv_body,
        grid=(t // tm,),
        in_specs=[
            pl.BlockSpec((tm, d), lambda i: (i, 0)),
            pl.BlockSpec((None, None, 6, d), lambda i: (layer, i // tiles_per_seq, 0, 0)),
            pl.BlockSpec((None, 1, d), lambda i: (layer, 0, 0)),
            pl.BlockSpec((None, d, n), lambda i: (layer, 0, 0), pipeline_mode=pl.Buffered(1)),
            pl.BlockSpec((tm, LANES), lambda i: (i % tiles_per_seq, 0)),
            pl.BlockSpec((tm, LANES), lambda i: (i % tiles_per_seq, 0)),
        ],
        out_specs=pl.BlockSpec((tm, n), lambda i: (i, 0)),
        out_shape=jax.ShapeDtypeStruct((t, n), jnp.bfloat16),
        scratch_shapes=[pltpu.VMEM((d, n), jnp.bfloat16)],
        compiler_params=pltpu.CompilerParams(
            dimension_semantics=("arbitrary",), vmem_limit_bytes=VMEM_LIMIT),
        name="qkv_proj",
    )(x2, mod, norm_g, w_in, cos_t, sin_t)


def _band_scores(q, k, mask):
    return [jnp.where(mask, _dot_nt(qh, k), NEG) for qh in _split_heads(q)]


def _band_finish(scores, v, head_lo):
    outs, lses = [], []
    for s in scores:
        m = jnp.max(s, axis=-1, keepdims=True)
        e = jnp.exp2(s - m)
        den = jnp.sum(e, axis=-1, keepdims=True)
        outs.append(_dot(_bf16(e), v) / den)
        lses.append(m + _log2(den))
    return jnp.where(head_lo, outs[0], outs[1]), jnp.where(head_lo, lses[0], lses[1])


def _dil_body(q1, k1, v1, q2, k2, v2, q3, k3, v3, o_ref, f_ref, og_ref, lg_ref, *, seq):
    head_lo = _iota((BAND, LANES), 1) < HEAD_DIM
    row = _iota((BAND, 2 * BAND), 0)
    col = _iota((BAND, 2 * BAND), 1)
    mask_win = (col >= row) & (col <= row + BAND)
    mask_own = _iota((BAND, BAND), 1) <= _iota((BAND, BAND), 0)

    srcs = ((q1, k1, v1), (q2, k2, v2), (q3, k3, v3))
    slot = {}
    for g, (_, dil) in enumerate(DIL_GROUPS):
        if dil > 1:
            for a in range(3):
                slot[g, a] = len(slot)
                f_ref[slot[g, a]] = _f32(srcs[g][a][...])

    units = []
    for g, (_, dil) in enumerate(DIL_GROUPS):
        sub_len = seq // dil
        for r in range(dil):
            if dil > 1:
                qs, ks, vs = (_bf16(f_ref[slot[g, a], pl.ds(r, sub_len, stride=dil), :]) for a in range(3))
            else:
                qs, ks, vs = (srcs[g][a][...] for a in range(3))
            for qb in range(sub_len // BAND):
                lo, hi = max(qb - 1, 0) * BAND, (qb + 1) * BAND
                units.append((g, r, qs[qb * BAND:hi], ks[lo:hi], vs[lo:hi], mask_own if qb == 0 else mask_win))

    scores, results = {}, {}
    for step in range(len(units) + DIL_AHEAD):
        if step < len(units):
            _, _, q, k, _, mask = units[step]
            scores[step] = _band_scores(q, k, mask)
        done = step - DIL_AHEAD
        if done >= 0:
            g, r, _, _, v, _ = units[done]
            results.setdefault((g, r), []).append(_band_finish(scores.pop(done), v, head_lo))
    for (g, r), blocks in results.items():
        dil = DIL_GROUPS[g][1]
        og_ref[g, pl.ds(r, seq // dil, stride=dil), :] = jnp.concatenate([o for o, _ in blocks], axis=0)
        lg_ref[g, pl.ds(r, seq // dil, stride=dil), :] = jnp.concatenate([l for _, l in blocks], axis=0)

    l1, l2, l3 = lg_ref[0], lg_ref[1], lg_ref[2]
    mx = jnp.maximum(jnp.maximum(l1, l2), l3)
    w1, w2, w3 = jnp.exp2(l1 - mx), jnp.exp2(l2 - mx), jnp.exp2(l3 - mx)
    o = (w1 * og_ref[0] + w2 * og_ref[1] + w3 * og_ref[2]) / (w1 + w2 + w3)
    o_ref[...] = _bf16(o)


def _dil_call(qkv3):
    b, seq, _ = qkv3.shape
    n_slot_pairs = HEADS_PER_DIL // 2

    def spec(base, g):
        return pl.BlockSpec((None, seq, LANES), lambda bi, p: (bi, 0, base + g * n_slot_pairs + p))

    in_specs = []
    for g in range(len(DIL_GROUPS)):
        in_specs += [spec(QR0, g), spec(KR0, g), spec(V0, g)]
    return pl.pallas_call(
        functools.partial(_dil_body, seq=seq),
        grid=(b, n_slot_pairs),
        in_specs=in_specs,
        out_specs=pl.BlockSpec((None, seq, LANES), lambda bi, p: (bi, 0, p)),
        out_shape=jax.ShapeDtypeStruct((b, seq, n_slot_pairs * LANES), jnp.bfloat16),
        scratch_shapes=[
            pltpu.VMEM((3 * sum(dil > 1 for _, dil in DIL_GROUPS), seq, LANES), jnp.float32),
            pltpu.VMEM((len(DIL_GROUPS), seq, LANES), jnp.float32),
            pltpu.VMEM((len(DIL_GROUPS), seq, LANES), jnp.float32),
        ],
        compiler_params=pltpu.CompilerParams(
            dimension_semantics=("arbitrary", "arbitrary"), vmem_limit_bytes=VMEM_LIMIT),
        name="dilated_attn",
    )(*([qkv3] * 9))


def _moba_prepare(q_ref, k_ref, v_ref, kmean_ref, qa_ref, ka_ref, va_ref, n_blk, first):
    v = v_ref[...]
    v_lo = _iota(v.shape, 1) < HEAD_DIM
    one = jnp.ones_like(v)
    va_ref[0] = jnp.where(v_lo, v, one)
    va_ref[1] = jnp.where(v_lo, one, v)
    if first >= n_blk:
        return

    kmean_ref[...] = jnp.zeros_like(kmean_ref)
    for j in range(n_blk):
        kj = _f32(k_ref[j * BLK:(j + 1) * BLK, :])
        kmean_ref[j:j + 1, :] = jnp.mean(kj, axis=0, keepdims=True)
    km = kmean_ref[...]
    km_hi = _bf16(km)
    km_mid = _bf16(km - _f32(km_hi))
    km_lo = _bf16(km - _f32(km_hi) - _f32(km_mid))

    q = q_ref[first * BLK:, :]
    k = k_ref[...]
    nq = q.shape[0]
    q_lo = _qk_head_lo(q.shape)
    k_lo = _qk_head_lo(k.shape)
    lane = _iota(k.shape, 1)
    key_blk = _iota(k.shape, 0) // BLK
    blk_id = _iota((KMEAN_ROWS, nq), 0)
    past = blk_id < _iota((KMEAN_ROWS, nq), 1) // BLK + first
    zero = jnp.zeros_like(q)
    pad = jnp.zeros((LANES - KMEAN_ROWS, nq), jnp.float32)
    for h in range(2):
        own = q_lo if h == 0 else ~q_lo
        qh = jnp.where(own, q, zero)
        gate = _dot_nt(km_hi, qh) + _dot_nt(km_mid, qh) + _dot_nt(km_lo, qh)
        sel = jnp.zeros(gate.shape, jnp.float32)
        for j in range(n_blk - 1):
            gj = gate[j:j + 1, :]
            beats = ((gate > gj) | ((gate == gj) & (blk_id < j))) & past
            rank = jnp.sum(jnp.where(beats, 1.0, 0.0), axis=0, keepdims=True)
            sel = jnp.where((blk_id == j) & (rank < MOBA_TOPK), 1.0, sel)
        bias = jnp.where(past & (sel < 0.5), NEG, 0.0)
        bias_t = jnp.concatenate([bias, pad], axis=0).T
        base = (1 - h) * (HEAD_DIM // 2)
        if base:
            bias_t = pltpu.roll(bias_t, base, 1)
        qa_ref[h] = jnp.where(own, q, _bf16(bias_t))
        ka_ref[h] = jnp.where(k_lo if h == 0 else ~k_lo, k,
                              jnp.where(lane - base == key_blk, 1.0, 0.0).astype(k.dtype))


def _moba_scores(c, h, first, q_ref, k_ref, qa_ref, ka_ref):
    n = (c + 1) * BLK
    if c < first:
        return _dot_nt(_split_heads(q_ref[c * BLK:n, :])[h], k_ref[0:n, :])
    return _dot_nt(qa_ref[h, (c - first) * BLK:(c - first + 1) * BLK, :], ka_ref[h, 0:n, :])


def _moba_finish(c, h, s, va_ref):
    n = (c + 1) * BLK
    causal = _iota((BLK, BLK), 1) <= _iota((BLK, BLK), 0)
    own = jnp.where(causal, s[:, c * BLK:], NEG)
    m = jnp.max(own, axis=-1, keepdims=True)
    if c > 0:
        left = s[:, :c * BLK]
        m = jnp.maximum(m, jnp.max(left, axis=-1, keepdims=True))
        p = jnp.concatenate([_bf16(jnp.exp2(left - m)), _bf16(jnp.exp2(own - m))], axis=1)
    else:
        p = _bf16(jnp.exp2(own - m))
    acc = _dot(p, va_ref[h, 0:n, :])
    return acc / pltpu.roll(acc, HEAD_DIM, 1)


def _moba_body(q_ref, k_ref, v_ref, o_ref, kmean_ref, qa_ref, ka_ref, va_ref, *, n_blk, first):
    _moba_prepare(q_ref, k_ref, v_ref, kmean_ref, qa_ref, ka_ref, va_ref, n_blk, first)
    units = [(c, h) for c in range(n_blk) for h in range(2)]
    outs, scores = {}, {}
    for step in range(len(units) + MOBA_AHEAD):
        if step < len(units):
            scores[step] = _moba_scores(*units[step], first, q_ref, k_ref, qa_ref, ka_ref)
        done = step - MOBA_AHEAD
        if done >= 0:
            outs[units[done]] = _moba_finish(*units[done], scores.pop(done), va_ref)
    head_lo = _iota((BLK, LANES), 1) < HEAD_DIM
    for c in range(n_blk):
        o_ref[c * BLK:(c + 1) * BLK, :] = _bf16(jnp.where(head_lo, outs[c, 0], outs[c, 1]))


def _moba_call(qkv3):
    b, seq, _ = qkv3.shape
    n_blk = seq // BLK
    first = min(MOBA_TOPK + 1, n_blk)
    n_sel = max(seq - first * BLK, BLK)
    return pl.pallas_call(
        functools.partial(_moba_body, n_blk=n_blk, first=first),
        grid=(b, NPB),
        in_specs=[
            pl.BlockSpec((None, seq, LANES), lambda bi, p: (bi, 0, QR0 + NPA + p)),
            pl.BlockSpec((None, seq, LANES), lambda bi, p: (bi, 0, KR0 + NPA + p)),
            pl.BlockSpec((None, seq, LANES), lambda bi, p: (bi, 0, V0 + NPA + p)),
        ],
        out_specs=pl.BlockSpec((None, seq, LANES), lambda bi, p: (bi, 0, p)),
        out_shape=jax.ShapeDtypeStruct((b, seq, NPB * LANES), jnp.bfloat16),
        scratch_shapes=[pltpu.VMEM((KMEAN_ROWS, LANES), jnp.float32),
                        pltpu.VMEM((2, n_sel, LANES), jnp.bfloat16),
                        pltpu.VMEM((2, seq, LANES), jnp.bfloat16),
                        pltpu.VMEM((2, seq, LANES), jnp.bfloat16)],
        compiler_params=pltpu.CompilerParams(
            dimension_semantics=("arbitrary", "arbitrary"), vmem_limit_bytes=VMEM_LIMIT),
        name="moba_attn",
    )(qkv3, qkv3, qkv3)


def _sb_unit(c, h, q_ref, k_ref, v_ref):
    n = (c + 1) * BLK
    row = _iota((BLK, BLK), 0)
    col = _iota((BLK, BLK), 1)
    past = col < row
    from_s = jnp.where(row >= col, 1.0, 0.0).astype(jnp.bfloat16)
    from_s2 = jnp.concatenate([from_s, from_s], axis=0)

    z_all = _dot_nt(_split_heads(q_ref[c * BLK:n, :])[h], k_ref[0:n, :])
    yield None

    z_parts, split_parts = [], []
    for j in range(c + 1):
        z = z_all[:, j * BLK:(j + 1) * BLK]
        sp = jnp.maximum(z, 0.0) + _log2(1.0 + jnp.exp2(-jnp.abs(z)))
        if j == c:
            sp = jnp.where(past, sp, 0.0)
        hi = _bf16(sp)
        lo = _bf16(sp - _f32(hi))
        split_parts.append(jnp.concatenate([hi, lo], axis=1))
        z_parts.append(z)
    tails = _dot(jnp.concatenate(split_parts, axis=0), from_s2)
    tail_parts = [tails[j * BLK:(j + 1) * BLK] for j in range(c + 1)]
    yield None

    a_parts = [None] * (c + 1)
    beyond = None
    for j in range(c, -1, -1):
        t = z_parts[j] - tail_parts[j]
        if beyond is not None:
            t = t - beyond
        a = jnp.exp2(t)
        if j == c:
            a = jnp.where(past, a, 0.0)
        a_parts[j] = _bf16(a)
        total = tail_parts[j][:, 0:1]
        beyond = total if beyond is None else beyond + total
    yield _dot(jnp.concatenate(a_parts, axis=1), v_ref[0:n, :])


def _sb_body(q_ref, k_ref, v_ref, o_ref, *, n_blk):
    units = [(c, h) for c in range(n_blk) for h in range(2)]
    gens = [_sb_unit(c, h, q_ref, k_ref, v_ref) for c, h in units]
    n_stages = 3
    accs = {}
    for step in range(len(units) + n_stages - 1):
        for stage in range(n_stages):
            u = step - stage
            if 0 <= u < len(units):
                res = next(gens[u])
                if stage == n_stages - 1:
                    accs[units[u]] = res
    head_lo = _iota((BLK, LANES), 1) < HEAD_DIM
    for c in range(n_blk):
        o_ref[c * BLK:(c + 1) * BLK, :] = _bf16(jnp.where(head_lo, accs[c, 0], accs[c, 1]))


def _sb_call(qkv3):
    b, seq, _ = qkv3.shape
    n_blk = seq // BLK
    return pl.pallas_call(
        functools.partial(_sb_body, n_blk=n_blk),
        grid=(b, NPC),
        in_specs=[
            pl.BlockSpec((None, seq, LANES), lambda bi, p: (bi, 0, QC0 + p)),
            pl.BlockSpec((None, seq, LANES), lambda bi, p: (bi, 0, KC0 + p)),
            pl.BlockSpec((None, seq, LANES), lambda bi, p: (bi, 0, V0 + NPA + NPB + p)),
        ],
        out_specs=pl.BlockSpec((None, seq, LANES), lambda bi, p: (bi, 0, p)),
        out_shape=jax.ShapeDtypeStruct((b, seq, NPC * LANES), jnp.bfloat16),
        compiler_params=pltpu.CompilerParams(
            dimension_semantics=("arbitrary", "arbitrary"), vmem_limit_bytes=VMEM_LIMIT),
        name="stickbreak_attn",
    )(qkv3, qkv3, qkv3)


def _merge_body(x_ref, mod_ref, g_ref, oa_ref, ob_ref, oc_ref, wa32_ref, wb32_ref, wc32_ref,
                wg32_ref, bg_ref, wo32_ref, out_ref, wa_ref, wb_ref, wc_ref, wg_ref, wo_ref):
    @pl.when(pl.program_id(0) == 0)
    def _():
        for src, dst in ((wa32_ref, wa_ref), (wb32_ref, wb_ref), (wc32_ref, wc_ref),
                         (wg32_ref, wg_ref), (wo32_ref, wo_ref)):
            dst[...] = _bf16(src[...])

    x = x_ref[...]
    d = x.shape[1]
    h = _bf16(_modulated_norm(x, g_ref[...], mod_ref[0:1, :], mod_ref[1:2, :]))
    merged = None
    for br, (o_ref, w_ref) in enumerate(((oa_ref, wa_ref), (ob_ref, wb_ref), (oc_ref, wc_ref))):
        gate = jax.nn.sigmoid(_dot(h, wg_ref[:, br * d:(br + 1) * d]) + bg_ref[:, br * d:(br + 1) * d])
        term = gate * _dot(o_ref[...], w_ref[...])
        merged = term if merged is None else merged + term
    out_ref[...] = x + mod_ref[2:3, :] * _dot(_bf16(merged), wo_ref[...])


def _merge_call(x2, mod, norm_g, oa, ob, oc, w_br_a, w_br_b, w_br_c, w_gate, b_gate, w_out, layer, seq):
    t, d = x2.shape
    tm = ROW_TILE
    tiles_per_seq = seq // tm

    def whole(arr):
        return pl.BlockSpec((None,) + arr.shape[1:], lambda i: (layer,) + (0,) * (arr.ndim - 1),
                            pipeline_mode=pl.Buffered(1))

    def rows(arr):
        return pl.BlockSpec((tm, arr.shape[1]), lambda i: (i, 0))

    return pl.pallas_call(
        _merge_body,
        grid=(t // tm,),
        in_specs=[
            rows(x2),
            pl.BlockSpec((None, None, 6, d), lambda i: (layer, i // tiles_per_seq, 0, 0)),
            whole(norm_g), rows(oa), rows(ob), rows(oc),
            whole(w_br_a), whole(w_br_b), whole(w_br_c), whole(w_gate), whole(b_gate), whole(w_out),
        ],
        out_specs=rows(x2),
        out_shape=jax.ShapeDtypeStruct((t, d), jnp.float32),
        scratch_shapes=[pltpu.VMEM(w.shape[1:], jnp.bfloat16)
                        for w in (w_br_a, w_br_b, w_br_c, w_gate, w_out)],
        compiler_params=pltpu.CompilerParams(
            dimension_semantics=("arbitrary",), vmem_limit_bytes=VMEM_LIMIT),
        name="gated_merge",
    )(x2, mod, norm_g, oa, ob, oc, w_br_a, w_br_b, w_br_c, w_gate, b_gate, w_out)


def _ffn_body(x_ref, mod_ref, g_ref, wg_ref, wu_ref, wd_ref, *rest):
    maybe_final_g_ref, out_ref = rest[:-1], rest[-1]
    x = x_ref[...]
    h = _bf16(_modulated_norm(x, g_ref[...], mod_ref[3:4, :], mod_ref[4:5, :]))
    gt = _dot(h, wg_ref[...])
    up = _dot(h, wu_ref[...])
    act = (gt * jax.nn.sigmoid(gt)) * up
    y = x + mod_ref[5:6, :] * _dot(_bf16(act), wd_ref[...])
    for final_g_ref in maybe_final_g_ref:
        y = _rms_norm(y, final_g_ref[...])
    out_ref[...] = y


def _ffn_call(x2, mod, norm_g, w_gu, w_down, layer, seq, final_g=None):
    t, d = x2.shape
    d_ff = w_down.shape[1]
    tm = ROW_TILE
    tiles_per_seq = seq // tm
    resident = pl.Buffered(1)
    in_specs = [
        pl.BlockSpec((tm, d), lambda i: (i, 0)),
        pl.BlockSpec((None, None, 6, d), lambda i: (layer, i // tiles_per_seq, 0, 0)),
        pl.BlockSpec((None, 1, d), lambda i: (layer, 0, 0)),
        pl.BlockSpec((None, d, d_ff), lambda i: (layer, 0, 0), pipeline_mode=resident),
        pl.BlockSpec((None, d, d_ff), lambda i: (layer, 0, 1), pipeline_mode=resident),
        pl.BlockSpec((None, d_ff, d), lambda i: (layer, 0, 0), pipeline_mode=resident),
    ]
    args = [x2, mod, norm_g, w_gu, w_gu, w_down]
    if final_g is not None:
        in_specs.append(pl.BlockSpec((1, d), lambda i: (0, 0)))
        args.append(final_g.reshape(1, d))
    return pl.pallas_call(
        _ffn_body,
        grid=(t // tm,),
        in_specs=in_specs,
        out_specs=pl.BlockSpec((tm, d), lambda i: (i, 0)),
        out_shape=jax.ShapeDtypeStruct((t, d), jnp.float32),
        compiler_params=pltpu.CompilerParams(
            dimension_semantics=("arbitrary",), vmem_limit_bytes=VMEM_LIMIT),
        name="swiglu_ffn",
    )(*args)


def _rope_tables(seq):
    pos = jnp.arange(seq, dtype=jnp.float32)
    inv = ROPE_THETA ** (-jnp.arange(0, HEAD_DIM, 2, dtype=jnp.float32) / HEAD_DIM)
    ang = pos[:, None] * inv[None, :]
    cos, sin = jnp.cos(ang), jnp.sin(ang)
    cos_t = jnp.tile(cos, (1, 2 * LANES // HEAD_DIM))
    sin_t = jnp.concatenate([-sin, -sin, sin, sin], axis=-1)
    return cos_t, sin_t


def kernel(x, c, w_ada, b_ada, norm1_g, w_in, w_br_a, w_br_b, w_br_c, w_gate, b_gate, w_out,
           norm2_g, w_gu, w_down, final_g):
    b, seq, d = x.shape
    depth = w_in.shape[0]
    cos_t, sin_t = _rope_tables(seq)
    w_gu, w_down = _bf16(w_gu), _bf16(w_down)
    b_gate3 = b_gate.reshape(depth, 1, -1)
    g1 = norm1_g.reshape(depth, 1, d)
    g2 = norm2_g.reshape(depth, 1, d)

    mod = _ada_call(c, w_ada, b_ada).reshape(depth, b, 6, d)
    x2 = x.reshape(b * seq, d)
    for layer in range(depth):
        qkv = _qkv_call(x2, mod, g1, w_in, cos_t, sin_t, layer, seq)
        qkv3 = qkv.reshape(b, seq, -1)
        oa = _dil_call(qkv3).reshape(b * seq, -1)
        ob = _moba_call(qkv3).reshape(b * seq, -1)
        oc = _sb_call(qkv3).reshape(b * seq, -1)
        x2 = _merge_call(x2, mod, g1, oa, ob, oc, w_br_a, w_br_b, w_br_c, w_gate, b_gate3, w_out,
                         layer, seq)
        x2 = _ffn_call(x2, mod, g2, w_gu, w_down, layer, seq,
                       final_g=final_g if layer == depth - 1 else None)
    return x2.reshape(b, seq, d)
```

```python
import functools

import jax
import jax.numpy as jnp
from jax import lax
from jax.experimental import pallas as pl
from jax.experimental.pallas import tpu as pltpu

HEAD_DIM = 64
LANES = 128
DIL_GROUPS = ((128, 1), (512, 4), (2048, 16))
HEADS_PER_DIL = 4
HA = len(DIL_GROUPS) * HEADS_PER_DIL
HB = 6
HC = 6
N_HEADS = HA + HB + HC
MIX_WIDTH = N_HEADS * HEAD_DIM
BAND = 128
DIL_AHEAD = 4
BLK = 256
MOBA_TOPK = 3
KMEAN_ROWS = 16
MOBA_AHEAD = 2
SB_SKEW = 2
ROPE_THETA = 10000.0
NORM_EPS = 1e-6
NEG = -1e30
LOG2E = 1.4426950408889634
SCORE_SCALE = HEAD_DIM ** -0.5 * LOG2E
VMEM_LIMIT = 56 * 1024 * 1024

NPA, NPB, NPC = HA // 2, HB // 2, HC // 2
QR0 = 0
KR0 = NPA + NPB
QC0 = 2 * (NPA + NPB)
KC0 = QC0 + NPC
V0 = KC0 + NPC
N_ROPE_COLS = 2 * (NPA + NPB) * LANES
MXU_COLS = 256
ROW_TILE = 512
ADA_COL_TILE = 1536


def _f32(x):
    return x.astype(jnp.float32)


def _bf16(x):
    return x.astype(jnp.bfloat16)


def _dot(a, b):
    return jnp.dot(a, b, preferred_element_type=jnp.float32)


def _dot_nt(a, b):
    return lax.dot_general(a, b, (((1,), (1,)), ((), ())), preferred_element_type=jnp.float32)


def _log2(x):
    return jnp.log(x) * LOG2E


def _iota(shape, dim):
    return lax.broadcasted_iota(jnp.int32, shape, dim)


def _rms_norm(x, g):
    ms = jnp.mean(x * x, axis=-1, keepdims=True)
    return (x * lax.rsqrt(ms + NORM_EPS)) * g


def _modulated_norm(x, g, shift, scale):
    return _rms_norm(x, g) * (1.0 + scale) + shift


def _qk_head_lo(shape):
    return (_iota(shape, 1) % HEAD_DIM) < (HEAD_DIM // 2)


def _split_heads(q):
    head_lo = _qk_head_lo(q.shape)
    zero = jnp.zeros_like(q)
    return jnp.where(head_lo, q, zero), jnp.where(head_lo, zero, q)


def _ada_body(c_ref, w_ref, b_ref, o_ref):
    c = c_ref[...]
    ca = c * jax.nn.sigmoid(c)
    o_ref[...] = jnp.dot(ca, w_ref[...], precision=lax.Precision.HIGHEST,
                         preferred_element_type=jnp.float32) + b_ref[...]


def _ada_call(c, w_ada, b_ada):
    depth, d, n = w_ada.shape
    b = c.shape[0]
    tn = ADA_COL_TILE
    return pl.pallas_call(
        _ada_body,
        grid=(depth, n // tn),
        in_specs=[
            pl.BlockSpec((b, d), lambda l, j: (0, 0)),
            pl.BlockSpec((None, d, tn), lambda l, j: (l, 0, j)),
            pl.BlockSpec((None, 1, tn), lambda l, j: (l, 0, j)),
        ],
        out_specs=pl.BlockSpec((None, b, tn), lambda l, j: (l, 0, j)),
        out_shape=jax.ShapeDtypeStruct((depth, b, n), jnp.float32),
        compiler_params=pltpu.CompilerParams(
            dimension_semantics=("arbitrary", "arbitrary"), vmem_limit_bytes=VMEM_LIMIT),
        name="ada_mod",
    )(c, w_ada, b_ada.reshape(depth, 1, n))


def _arrange_qkv_weight(w_ref, wb_ref):
    n_rot_pairs = NPA + NPB
    lane = _iota((w_ref.shape[0], LANES), 1)
    keep = (lane < HEAD_DIM // 2) | (lane >= LANES - HEAD_DIM // 2)
    from_hi = lane < HEAD_DIM

    def pair(src_col, scale):
        blk = w_ref[:, src_col:src_col + LANES]
        if scale is not None:
            blk = blk * scale
        moved = jnp.where(from_hi, pltpu.roll(blk, LANES - HEAD_DIM // 2, 1), pltpu.roll(blk, HEAD_DIM // 2, 1))
        return _bf16(jnp.where(keep, blk, moved))

    dst = 0
    for part, first_pair, n_pairs in ((0, 0, n_rot_pairs), (1, 0, n_rot_pairs),
                                     (0, n_rot_pairs, NPC), (1, n_rot_pairs, NPC)):
        for p in range(first_pair, first_pair + n_pairs):
            wb_ref[:, dst:dst + LANES] = pair(part * MIX_WIDTH + p * LANES, SCORE_SCALE if part == 0 else None)
            dst += LANES
    wb_ref[:, dst:] = _bf16(w_ref[:, 2 * MIX_WIDTH:])


def _qkv_body(x_ref, mod_ref, g_ref, w_ref, cos_ref, sin_ref, o_ref, wb_ref):
    @pl.when(pl.program_id(0) == 0)
    def _():
        _arrange_qkv_weight(w_ref, wb_ref)

    h = _bf16(_modulated_norm(x_ref[...], g_ref[...], mod_ref[0:1, :], mod_ref[1:2, :]))
    cos = cos_ref[...]
    sin = sin_ref[...]
    for n in range(wb_ref.shape[1] // MXU_COLS):
        res = _dot(h, wb_ref[:, n * MXU_COLS:(n + 1) * MXU_COLS])
        for g in range(MXU_COLS // LANES):
            lo = n * MXU_COLS + g * LANES
            blk = res[:, g * LANES:(g + 1) * LANES]
            if lo < N_ROPE_COLS:
                blk = blk * cos + pltpu.roll(blk, HEAD_DIM, 1) * sin
            o_ref[:, lo:lo + LANES] = _bf16(blk)


def _qkv_call(x2, mod, norm_g, w_in, cos_t, sin_t, layer, seq):
    t, d = x2.shape
    n = w_in.shape[2]
    tm = ROW_TILE
    tiles_per_seq = seq // tm
    return pl.pallas_call(
        _qkv_body,
        grid=(t // tm,),
        in_specs=[
            pl.BlockSpec((tm, d), lambda i: (i, 0)),
            pl.BlockSpec((None, None, 6, d), lambda i: (layer, i // tiles_per_seq, 0, 0)),
            pl.BlockSpec((None, 1, d), lambda i: (layer, 0, 0)),
            pl.BlockSpec((None, d, n), lambda i: (layer, 0, 0), pipeline_mode=pl.Buffered(1)),
            pl.BlockSpec((tm, LANES), lambda i: (i % tiles_per_seq, 0)),
            pl.BlockSpec((tm, LANES), lambda i: (i % tiles_per_seq, 0)),
        ],
        out_specs=pl.BlockSpec((tm, n), lambda i: (i, 0)),
        out_shape=jax.ShapeDtypeStruct((t, n), jnp.bfloat16),
        scratch_shapes=[pltpu.VMEM((d, n), jnp.bfloat16)],
        compiler_params=pltpu.CompilerParams(
            dimension_semantics=("arbitrary",), vmem_limit_bytes=VMEM_LIMIT),
        name="qkv_proj",
    )(x2, mod, norm_g, w_in, cos_t, sin_t)


def _band_scores(q, k, mask):
    return [jnp.where(mask, _dot_nt(qh, k), NEG) for qh in _split_heads(q)]


def _band_finish(scores, v, head_lo):
    outs, lses = [], []
    for s in scores:
        m = jnp.max(s, axis=-1, keepdims=True)
        e = jnp.exp2(s - m)
        den = jnp.sum(e, axis=-1, keepdims=True)
        outs.append(_dot(_bf16(e), v) / den)
        lses.append(m + _log2(den))
    return jnp.where(head_lo, outs[0], outs[1]), jnp.where(head_lo, lses[0], lses[1])


def _dil_body(q1, k1, v1, q2, k2, v2, q3, k3, v3, o_ref, f_ref, og_ref, lg_ref, *, seq):
    head_lo = _iota((BAND, LANES), 1) < HEAD_DIM
    row = _iota((BAND, 2 * BAND), 0)
    col = _iota((BAND, 2 * BAND), 1)
    mask_win = (col >= row) & (col <= row + BAND)
    mask_own = _iota((BAND, BAND), 1) <= _iota((BAND, BAND), 0)

    srcs = ((q1, k1, v1), (q2, k2, v2), (q3, k3, v3))
    slot = {}
    for g, (_, dil) in enumerate(DIL_GROUPS):
        if dil > 1:
            for a in range(3):
                slot[g, a] = len(slot)
                f_ref[slot[g, a]] = _f32(srcs[g][a][...])

    units = []
    for g, (_, dil) in enumerate(DIL_GROUPS):
        sub_len = seq // dil
        for r in range(dil):
            if dil > 1:
                qs, ks, vs = (_bf16(f_ref[slot[g, a], pl.ds(r, sub_len, stride=dil), :]) for a in range(3))
            else:
                qs, ks, vs = (srcs[g][a][...] for a in range(3))
            for qb in range(sub_len // BAND):
                lo, hi = max(qb - 1, 0) * BAND, (qb + 1) * BAND
                units.append((g, r, qs[qb * BAND:hi], ks[lo:hi], vs[lo:hi], mask_own if qb == 0 else mask_win))

    scores, results = {}, {}
    for step in range(len(units) + DIL_AHEAD):
        if step < len(units):
            _, _, q, k, _, mask = units[step]
            scores[step] = _band_scores(q, k, mask)
        done = step - DIL_AHEAD
        if done >= 0:
            g, r, _, _, v, _ = units[done]
            results.setdefault((g, r), []).append(_band_finish(scores.pop(done), v, head_lo))
    for (g, r), blocks in results.items():
        dil = DIL_GROUPS[g][1]
        og_ref[g, pl.ds(r, seq // dil, stride=dil), :] = jnp.concatenate([o for o, _ in blocks], axis=0)
        lg_ref[g, pl.ds(r, seq // dil, stride=dil), :] = jnp.concatenate([l for _, l in blocks], axis=0)

    l1, l2, l3 = lg_ref[0], lg_ref[1], lg_ref[2]
    mx = jnp.maximum(jnp.maximum(l1, l2), l3)
    w1, w2, w3 = jnp.exp2(l1 - mx), jnp.exp2(l2 - mx), jnp.exp2(l3 - mx)
    o = (w1 * og_ref[0] + w2 * og_ref[1] + w3 * og_ref[2]) / (w1 + w2 + w3)
    o_ref[...] = _bf16(o)


def _dil_call(qkv3):
    b, seq, _ = qkv3.shape
    n_slot_pairs = HEADS_PER_DIL // 2

    def spec(base, g):
        return pl.BlockSpec((None, seq, LANES), lambda bi, p: (bi, 0, base + g * n_slot_pairs + p))

    in_specs = []
    for g in range(len(DIL_GROUPS)):
        in_specs += [spec(QR0, g), spec(KR0, g), spec(V0, g)]
    return pl.pallas_call(
        functools.partial(_dil_body, seq=seq),
        grid=(b, n_slot_pairs),
        in_specs=in_specs,
        out_specs=pl.BlockSpec((None, seq, LANES), lambda bi, p: (bi, 0, p)),
        out_shape=jax.ShapeDtypeStruct((b, seq, n_slot_pairs * LANES), jnp.bfloat16),
        scratch_shapes=[
            pltpu.VMEM((3 * sum(dil > 1 for _, dil in DIL_GROUPS), seq, LANES), jnp.float32),
            pltpu.VMEM((len(DIL_GROUPS), seq, LANES), jnp.float32),
            pltpu.VMEM((len(DIL_GROUPS), seq, LANES), jnp.float32),
        ],
        compiler_params=pltpu.CompilerParams(
            dimension_semantics=("arbitrary", "arbitrary"), vmem_limit_bytes=VMEM_LIMIT),
        name="dilated_attn",
    )(*([qkv3] * 9))


def _moba_prepare(q_ref, k_ref, v_ref, kmean_ref, qa_ref, ka_ref, va_ref, n_blk, first):
    v = v_ref[...]
    v_lo = _iota(v.shape, 1) < HEAD_DIM
    one = jnp.ones_like(v)
    va_ref[0] = jnp.where(v_lo, v, one)
    va_ref[1] = jnp.where(v_lo, one, v)
    if first >= n_blk:
        return

    kmean_ref[...] = jnp.zeros_like(kmean_ref)
    for j in range(n_blk):
        kj = _f32(k_ref[j * BLK:(j + 1) * BLK, :])
        kmean_ref[j:j + 1, :] = jnp.mean(kj, axis=0, keepdims=True)
    km = kmean_ref[...]
    km_hi = _bf16(km)
    km_mid = _bf16(km - _f32(km_hi))
    km_lo = _bf16(km - _f32(km_hi) - _f32(km_mid))

    q = q_ref[first * BLK:, :]
    k = k_ref[...]
    nq = q.shape[0]
    q_lo = _qk_head_lo(q.shape)
    k_lo = _qk_head_lo(k.shape)
    lane = _iota(k.shape, 1)
    key_blk = _iota(k.shape, 0) // BLK
    blk_id = _iota((KMEAN_ROWS, nq), 0)
    past = blk_id < _iota((KMEAN_ROWS, nq), 1) // BLK + first
    zero = jnp.zeros_like(q)
    pad = jnp.zeros((LANES - KMEAN_ROWS, nq), jnp.float32)
    for h in range(2):
        own = q_lo if h == 0 else ~q_lo
        qh = jnp.where(own, q, zero)
        gate = _dot_nt(km_hi, qh) + _dot_nt(km_mid, qh) + _dot_nt(km_lo, qh)
        sel = jnp.zeros(gate.shape, jnp.float32)
        for j in range(n_blk - 1):
            gj = gate[j:j + 1, :]
            beats = ((gate > gj) | ((gate == gj) & (blk_id < j))) & past
            rank = jnp.sum(jnp.where(beats, 1.0, 0.0), axis=0, keepdims=True)
            sel = jnp.where((blk_id == j) & (rank < MOBA_TOPK), 1.0, sel)
        bias = jnp.where(past & (sel < 0.5), NEG, 0.0)
        bias_t = jnp.concatenate([bias, pad], axis=0).T
        base = (1 - h) * (HEAD_DIM // 2)
        if base:
            bias_t = pltpu.roll(bias_t, base, 1)
        qa_ref[h] = jnp.where(own, q, _bf16(bias_t))
        ka_ref[h] = jnp.where(k_lo if h == 0 else ~k_lo, k,
                              jnp.where(lane - base == key_blk, 1.0, 0.0).astype(k.dtype))


def _moba_scores(c, h, first, q_ref, k_ref, qa_ref, ka_ref):
    n = (c + 1) * BLK
    if c < first:
        return _dot_nt(_split_heads(q_ref[c * BLK:n, :])[h], k_ref[0:n, :])
    return _dot_nt(qa_ref[h, (c - first) * BLK:(c - first + 1) * BLK, :], ka_ref[h, 0:n, :])


def _moba_finish(c, h, s, va_ref):
    n = (c + 1) * BLK
    causal = _iota((BLK, BLK), 1) <= _iota((BLK, BLK), 0)
    own = jnp.where(causal, s[:, c * BLK:], NEG)
    m = jnp.max(own, axis=-1, keepdims=True)
    if c > 0:
        left = s[:, :c * BLK]
        m = jnp.maximum(m, jnp.max(left, axis=-1, keepdims=True))
        p = jnp.concatenate([_bf16(jnp.exp2(left - m)), _bf16(jnp.exp2(own - m))], axis=1)
    else:
        p = _bf16(jnp.exp2(own - m))
    acc = _dot(p, va_ref[h, 0:n, :])
    return acc / pltpu.roll(acc, HEAD_DIM, 1)


def _moba_body(q_ref, k_ref, v_ref, o_ref, kmean_ref, qa_ref, ka_ref, va_ref, *, n_blk, first):
    _moba_prepare(q_ref, k_ref, v_ref, kmean_ref, qa_ref, ka_ref, va_ref, n_blk, first)
    units = [(c, h) for c in range(n_blk) for h in range(2)]
    outs, scores = {}, {}
    for step in range(len(units) + MOBA_AHEAD):
        if step < len(units):
            scores[step] = _moba_scores(*units[step], first, q_ref, k_ref, qa_ref, ka_ref)
        done = step - MOBA_AHEAD
        if done >= 0:
            outs[units[done]] = _moba_finish(*units[done], scores.pop(done), va_ref)
    head_lo = _iota((BLK, LANES), 1) < HEAD_DIM
    for c in range(n_blk):
        o_ref[c * BLK:(c + 1) * BLK, :] = _bf16(jnp.where(head_lo, outs[c, 0], outs[c, 1]))


def _moba_call(qkv3):
    b, seq, _ = qkv3.shape
    n_blk = seq // BLK
    first = min(MOBA_TOPK + 1, n_blk)
    n_sel = max(seq - first * BLK, BLK)
    return pl.pallas_call(
        functools.partial(_moba_body, n_blk=n_blk, first=first),
        grid=(b, NPB),
        in_specs=[
            pl.BlockSpec((None, seq, LANES), lambda bi, p: (bi, 0, QR0 + NPA + p)),
            pl.BlockSpec((None, seq, LANES), lambda bi, p: (bi, 0, KR0 + NPA + p)),
            pl.BlockSpec((None, seq, LANES), lambda bi, p: (bi, 0, V0 + NPA + p)),
        ],
        out_specs=pl.BlockSpec((None, seq, LANES), lambda bi, p: (bi, 0, p)),
        out_shape=jax.ShapeDtypeStruct((b, seq, NPB * LANES), jnp.bfloat16),
        scratch_shapes=[pltpu.VMEM((KMEAN_ROWS, LANES), jnp.float32),
                        pltpu.VMEM((2, n_sel, LANES), jnp.bfloat16),
                        pltpu.VMEM((2, seq, LANES), jnp.bfloat16),
                        pltpu.VMEM((2, seq, LANES), jnp.bfloat16)],
        compiler_params=pltpu.CompilerParams(
            dimension_semantics=("arbitrary", "arbitrary"), vmem_limit_bytes=VMEM_LIMIT),
        name="moba_attn",
    )(qkv3, qkv3, qkv3)


def _sb_tile(c, h, j, state, q_ref, k_ref, v_ref):
    row = _iota((BLK, BLK), 0)
    col = _iota((BLK, BLK), 1)
    past = col < row
    from_s = jnp.where(row >= col, 1.0, 0.0).astype(jnp.bfloat16)
    from_s2 = jnp.concatenate([from_s, from_s], axis=0)

    qh = _split_heads(q_ref[c * BLK:(c + 1) * BLK, :])[h]
    z = _dot_nt(qh, k_ref[j * BLK:(j + 1) * BLK, :])
    yield

    sp = jnp.maximum(z, 0.0) + _log2(1.0 + jnp.exp2(-jnp.abs(z)))
    if j == c:
        sp = jnp.where(past, sp, 0.0)
    hi = _bf16(sp)
    lo = _bf16(sp - _f32(hi))
    tail = _dot(jnp.concatenate([hi, lo], axis=1), from_s2)
    beyond = state.get("beyond")
    total = tail[:, 0:1]
    state["beyond"] = total if beyond is None else beyond + total
    yield

    t = z - tail
    if beyond is not None:
        t = t - beyond
    a = jnp.exp2(t)
    if j == c:
        a = jnp.where(past, a, 0.0)
    part = _dot(_bf16(a), v_ref[j * BLK:(j + 1) * BLK, :])
    state["acc"] = part if "acc" not in state else state["acc"] + part
    yield


def _sb_body(q_ref, k_ref, v_ref, o_ref, *, n_blk):
    states = {(c, h): {} for c in range(n_blk) for h in range(2)}
    tiles = [(c, h, j) for c in range(n_blk) for h in range(2) for j in range(c, -1, -1)]
    gens = [_sb_tile(c, h, j, states[c, h], q_ref, k_ref, v_ref) for c, h, j in tiles]
    n_stages = 3
    for step in range(len(tiles) + (n_stages - 1) * SB_SKEW):
        for stage in range(n_stages):
            t = step - stage * SB_SKEW
            if 0 <= t < len(tiles):
                next(gens[t])
    head_lo = _iota((BLK, LANES), 1) < HEAD_DIM
    for c in range(n_blk):
        o_ref[c * BLK:(c + 1) * BLK, :] = _bf16(jnp.where(head_lo, states[c, 0]["acc"], states[c, 1]["acc"]))


def _sb_call(qkv3):
    b, seq, _ = qkv3.shape
    n_blk = seq // BLK
    return pl.pallas_call(
        functools.partial(_sb_body, n_blk=n_blk),
        grid=(b, NPC),
        in_specs=[
            pl.BlockSpec((None, seq, LANES), lambda bi, p: (bi, 0, QC0 + p)),
            pl.BlockSpec((None, seq, LANES), lambda bi, p: (bi, 0, KC0 + p)),
            pl.BlockSpec((None, seq, LANES), lambda bi, p: (bi, 0, V0 + NPA + NPB + p)),
        ],
        out_specs=pl.BlockSpec((None, seq, LANES), lambda bi, p: (bi, 0, p)),
        out_shape=jax.ShapeDtypeStruct((b, seq, NPC * LANES), jnp.bfloat16),
        compiler_params=pltpu.CompilerParams(
            dimension_semantics=("arbitrary", "arbitrary"), vmem_limit_bytes=VMEM_LIMIT),
        name="stickbreak_attn",
    )(qkv3, qkv3, qkv3)


def _merge_body(x_ref, mod_ref, g_ref, oa_ref, ob_ref, oc_ref, wa32_ref, wb32_ref, wc32_ref,
                wg32_ref, bg_ref, wo32_ref, out_ref, wa_ref, wb_ref, wc_ref, wg_ref, wo_ref):
    @pl.when(pl.program_id(0) == 0)
    def _():
        for src, dst in ((wa32_ref, wa_ref), (wb32_ref, wb_ref), (wc32_ref, wc_ref),
                         (wg32_ref, wg_ref), (wo32_ref, wo_ref)):
            dst[...] = _bf16(src[...])

    x = x_ref[...]
    d = x.shape[1]
    h = _bf16(_modulated_norm(x, g_ref[...], mod_ref[0:1, :], mod_ref[1:2, :]))
    merged = None
    for br, (o_ref, w_ref) in enumerate(((oa_ref, wa_ref), (ob_ref, wb_ref), (oc_ref, wc_ref))):
        gate = jax.nn.sigmoid(_dot(h, wg_ref[:, br * d:(br + 1) * d]) + bg_ref[:, br * d:(br + 1) * d])
        term = gate * _dot(o_ref[...], w_ref[...])
        merged = term if merged is None else merged + term
    out_ref[...] = x + mod_ref[2:3, :] * _dot(_bf16(merged), wo_ref[...])


def _merge_call(x2, mod, norm_g, oa, ob, oc, w_br_a, w_br_b, w_br_c, w_gate, b_gate, w_out, layer, seq):
    t, d = x2.shape
    tm = ROW_TILE
    tiles_per_seq = seq // tm

    def whole(arr):
        return pl.BlockSpec((None,) + arr.shape[1:], lambda i: (layer,) + (0,) * (arr.ndim - 1),
                            pipeline_mode=pl.Buffered(1))

    def rows(arr):
        return pl.BlockSpec((tm, arr.shape[1]), lambda i: (i, 0))

    return pl.pallas_call(
        _merge_body,
        grid=(t // tm,),
        in_specs=[
            rows(x2),
            pl.BlockSpec((None, None, 6, d), lambda i: (layer, i // tiles_per_seq, 0, 0)),
            whole(norm_g), rows(oa), rows(ob), rows(oc),
            whole(w_br_a), whole(w_br_b), whole(w_br_c), whole(w_gate), whole(b_gate), whole(w_out),
        ],
        out_specs=rows(x2),
        out_shape=jax.ShapeDtypeStruct((t, d), jnp.float32),
        scratch_shapes=[pltpu.VMEM(w.shape[1:], jnp.bfloat16)
                        for w in (w_br_a, w_br_b, w_br_c, w_gate, w_out)],
        compiler_params=pltpu.CompilerParams(
            dimension_semantics=("arbitrary",), vmem_limit_bytes=VMEM_LIMIT),
        name="gated_merge",
    )(x2, mod, norm_g, oa, ob, oc, w_br_a, w_br_b, w_br_c, w_gate, b_gate, w_out)


def _ffn_body(x_ref, mod_ref, g_ref, wg_ref, wu_ref, wd_ref, *rest):
    maybe_final_g_ref, out_ref = rest[:-1], rest[-1]
    x = x_ref[...]
    h = _bf16(_modulated_norm(x, g_ref[...], mod_ref[3:4, :], mod_ref[4:5, :]))
    gt = _dot(h, wg_ref[...])
    up = _dot(h, wu_ref[...])
    act = (gt * jax.nn.sigmoid(gt)) * up
    y = x + mod_ref[5:6, :] * _dot(_bf16(act), wd_ref[...])
    for final_g_ref in maybe_final_g_ref:
        y = _rms_norm(y, final_g_ref[...])
    out_ref[...] = y


def _ffn_call(x2, mod, norm_g, w_gu, w_down, layer, seq, final_g=None):
    t, d = x2.shape
    d_ff = w_down.shape[1]
    tm = ROW_TILE
    tiles_per_seq = seq // tm
    resident = pl.Buffered(1)
    in_specs = [
        pl.BlockSpec((tm, d), lambda i: (i, 0)),
        pl.BlockSpec((None, None, 6, d), lambda i: (layer, i // tiles_per_seq, 0, 0)),
        pl.BlockSpec((None, 1, d), lambda i: (layer, 0, 0)),
        pl.BlockSpec((None, d, d_ff), lambda i: (layer, 0, 0), pipeline_mode=resident),
        pl.BlockSpec((None, d, d_ff), lambda i: (layer, 0, 1), pipeline_mode=resident),
        pl.BlockSpec((None, d_ff, d), lambda i: (layer, 0, 0), pipeline_mode=resident),
    ]
    args = [x2, mod, norm_g, w_gu, w_gu, w_down]
    if final_g is not None:
        in_specs.append(pl.BlockSpec((1, d), lambda i: (0, 0)))
        args.append(final_g.reshape(1, d))
    return pl.pallas_call(
        _ffn_body,
        grid=(t // tm,),
        in_specs=in_specs,
        out_specs=pl.BlockSpec((tm, d), lambda i: (i, 0)),
        out_shape=jax.ShapeDtypeStruct((t, d), jnp.float32),
        compiler_params=pltpu.CompilerParams(
            dimension_semantics=("arbitrary",), vmem_limit_bytes=VMEM_LIMIT),
        name="swiglu_ffn",
    )(*args)


def _rope_tables(seq):
    pos = jnp.arange(seq, dtype=jnp.float32)
    inv = ROPE_THETA ** (-jnp.arange(0, HEAD_DIM, 2, dtype=jnp.float32) / HEAD_DIM)
    ang = pos[:, None] * inv[None, :]
    cos, sin = jnp.cos(ang), jnp.sin(ang)
    cos_t = jnp.tile(cos, (1, 2 * LANES // HEAD_DIM))
    sin_t = jnp.concatenate([-sin, -sin, sin, sin], axis=-1)
    return cos_t, sin_t


def kernel(x, c, w_ada, b_ada, norm1_g, w_in, w_br_a, w_br_b, w_br_c, w_gate, b_gate, w_out,
           norm2_g, w_gu, w_down, final_g):
    b, seq, d = x.shape
    depth = w_in.shape[0]
    cos_t, sin_t = _rope_tables(seq)
    w_gu, w_down = _bf16(w_gu), _bf16(w_down)
    b_gate3 = b_gate.reshape(depth, 1, -1)
    g1 = norm1_g.reshape(depth, 1, d)
    g2 = norm2_g.reshape(depth, 1, d)

    mod = _ada_call(c, w_ada, b_ada).reshape(depth, b, 6, d)
    x2 = x.reshape(b * seq, d)
    for layer in range(depth):
        qkv = _qkv_call(x2, mod, g1, w_in, cos_t, sin_t, layer, seq)
        qkv3 = qkv.reshape(b, seq, -1)
        oa = _dil_call(qkv3).reshape(b * seq, -1)
        ob = _moba_call(qkv3).reshape(b * seq, -1)
        oc = _sb_call(qkv3).reshape(b * seq, -1)
        x2 = _merge_call(x2, mod, g1, oa, ob, oc, w_br_a, w_br_b, w_br_c, w_gate, b_gate3, w_out,
                         layer, seq)
        x2 = _ffn_call(x2, mod, g2, w_gu, w_down, layer, seq,
                       final_g=final_g if layer == depth - 1 else None)
    return x2.reshape(b, seq, d)
```

```python
import functools

import jax
import jax.numpy as jnp
from jax import lax
from jax.experimental import pallas as pl
from jax.experimental.pallas import tpu as pltpu

HEAD_DIM = 64
LANES = 128
DIL_GROUPS = ((128, 1), (512, 4), (2048, 16))
HEADS_PER_DIL = 4
HA = len(DIL_GROUPS) * HEADS_PER_DIL
HB = 6
HC = 6
N_HEADS = HA + HB + HC
MIX_WIDTH = N_HEADS * HEAD_DIM
BAND = 128
DIL_AHEAD = 4
BLK = 256
MOBA_TOPK = 3
KMEAN_ROWS = 16
MOBA_AHEAD = 2
SB_SKEW = 2
ROPE_THETA = 10000.0
NORM_EPS = 1e-6
NEG = -1e30
LOG2E = 1.4426950408889634
SCORE_SCALE = HEAD_DIM ** -0.5 * LOG2E
VMEM_LIMIT = 56 * 1024 * 1024

NPA, NPB, NPC = HA // 2, HB // 2, HC // 2
QR0 = 0
KR0 = NPA + NPB
QC0 = 2 * (NPA + NPB)
KC0 = QC0 + NPC
V0 = KC0 + NPC
N_ROPE_COLS = 2 * (NPA + NPB) * LANES
MXU_COLS = 256
ROW_TILE = 512
ADA_COL_TILE = 1536


def _f32(x):
    return x.astype(jnp.float32)


def _bf16(x):
    return x.astype(jnp.bfloat16)


def _dot(a, b):
    return jnp.dot(a, b, preferred_element_type=jnp.float32)


def _dot_nt(a, b):
    return lax.dot_general(a, b, (((1,), (1,)), ((), ())), preferred_element_type=jnp.float32)


def _log2(x):
    return jnp.log(x) * LOG2E


def _iota(shape, dim):
    return lax.broadcasted_iota(jnp.int32, shape, dim)


def _rms_norm(x, g):
    ms = jnp.mean(x * x, axis=-1, keepdims=True)
    return (x * lax.rsqrt(ms + NORM_EPS)) * g


def _modulated_norm(x, g, shift, scale):
    return _rms_norm(x, g) * (1.0 + scale) + shift


def _qk_head_lo(shape):
    return (_iota(shape, 1) % HEAD_DIM) < (HEAD_DIM // 2)


def _split_heads(q):
    head_lo = _qk_head_lo(q.shape)
    zero = jnp.zeros_like(q)
    return jnp.where(head_lo, q, zero), jnp.where(head_lo, zero, q)


def _ada_body(c_ref, w_ref, b_ref, o_ref):
    c = c_ref[...]
    ca = c * jax.nn.sigmoid(c)
    o_ref[...] = jnp.dot(ca, w_ref[...], precision=lax.Precision.HIGHEST,
                         preferred_element_type=jnp.float32) + b_ref[...]


def _ada_call(c, w_ada, b_ada):
    depth, d, n = w_ada.shape
    b = c.shape[0]
    tn = ADA_COL_TILE
    return pl.pallas_call(
        _ada_body,
        grid=(depth, n // tn),
        in_specs=[
            pl.BlockSpec((b, d), lambda l, j: (0, 0)),
            pl.BlockSpec((None, d, tn), lambda l, j: (l, 0, j)),
            pl.BlockSpec((None, 1, tn), lambda l, j: (l, 0, j)),
        ],
        out_specs=pl.BlockSpec((None, b, tn), lambda l, j: (l, 0, j)),
        out_shape=jax.ShapeDtypeStruct((depth, b, n), jnp.float32),
        compiler_params=pltpu.CompilerParams(
            dimension_semantics=("arbitrary", "arbitrary"), vmem_limit_bytes=VMEM_LIMIT),
        name="ada_mod",
    )(c, w_ada, b_ada.reshape(depth, 1, n))


def _arrange_qkv_weight(w_ref, wb_ref):
    n_rot_pairs = NPA + NPB
    lane = _iota((w_ref.shape[0], LANES), 1)
    keep = (lane < HEAD_DIM // 2) | (lane >= LANES - HEAD_DIM // 2)
    from_hi = lane < HEAD_DIM

    def pair(src_col, scale):
        blk = w_ref[:, src_col:src_col + LANES]
        if scale is not None:
            blk = blk * scale
        moved = jnp.where(from_hi, pltpu.roll(blk, LANES - HEAD_DIM // 2, 1), pltpu.roll(blk, HEAD_DIM // 2, 1))
        return _bf16(jnp.where(keep, blk, moved))

    dst = 0
    for part, first_pair, n_pairs in ((0, 0, n_rot_pairs), (1, 0, n_rot_pairs),
                                     (0, n_rot_pairs, NPC), (1, n_rot_pairs, NPC)):
        for p in range(first_pair, first_pair + n_pairs):
            wb_ref[:, dst:dst + LANES] = pair(part * MIX_WIDTH + p * LANES, SCORE_SCALE if part == 0 else None)
            dst += LANES
    wb_ref[:, dst:] = _bf16(w_ref[:, 2 * MIX_WIDTH:])


def _qkv_body(x_ref, mod_ref, g_ref, w_ref, cos_ref, sin_ref, o_ref, wb_ref):
    @pl.when(pl.program_id(0) == 0)
    def _():
        _arrange_qkv_weight(w_ref, wb_ref)

    h = _bf16(_modulated_norm(x_ref[...], g_ref[...], mod_ref[0:1, :], mod_ref[1:2, :]))
    cos = cos_ref[...]
    sin = sin_ref[...]
    for n in range(wb_ref.shape[1] // MXU_COLS):
        res = _dot(h, wb_ref[:, n * MXU_COLS:(n + 1) * MXU_COLS])
        for g in range(MXU_COLS // LANES):
            lo = n * MXU_COLS + g * LANES
            blk = res[:, g * LANES:(g + 1) * LANES]
            if lo < N_ROPE_COLS:
                blk = blk * cos + pltpu.roll(blk, HEAD_DIM, 1) * sin
            o_ref[:, lo:lo + LANES] = _bf16(blk)


def _qkv_call(x2, mod, norm_g, w_in, cos_t, sin_t, layer, seq):
    t, d = x2.shape
    n = w_in.shape[2]
    tm = ROW_TILE
    tiles_per_seq = seq // tm
    return pl.pallas_call(
        _qkv_body,
        grid=(t // tm,),
        in_specs=[
            pl.BlockSpec((tm, d), lambda i: (i, 0)),
            pl.BlockSpec((None, None, 6, d), lambda i: (layer, i // tiles_per_seq, 0, 0)),
            pl.BlockSpec((None, 1, d), lambda i: (layer, 0, 0)),
            pl.BlockSpec((None, d, n), lambda i: (layer, 0, 0), pipeline_mode=pl.Buffered(1)),
            pl.BlockSpec((tm, LANES), lambda i: (i % tiles_per_seq, 0)),
            pl.BlockSpec((tm, LANES), lambda i: (i % tiles_per_seq, 0)),
        ],
        out_specs=pl.BlockSpec((tm, n), lambda i: (i, 0)),
        out_shape=jax.ShapeDtypeStruct((t, n), jnp.bfloat16),
        scratch_shapes=[pltpu.VMEM((d, n), jnp.bfloat16)],
        compiler_params=pltpu.CompilerParams(
            dimension_semantics=("arbitrary",), vmem_limit_bytes=VMEM_LIMIT),
        name="qkv_proj",
    )(x2, mod, norm_g, w_in, cos_t, sin_t)


def _band_finish(s, v):
    m = jnp.max(s, axis=-1, keepdims=True)
    e = jnp.exp2(s - m)
    den = jnp.sum(e, axis=-1, keepdims=True)
    return _dot(_bf16(e), v) / den, m + _log2(den)


def _dil_body(q1, k1, v1, q2, k2, v2, q3, k3, v3, o_ref, f_ref, og_ref, lg_ref, *, seq):
    head_lo = _iota((BAND, LANES), 1) < HEAD_DIM
    row = _iota((BAND, 2 * BAND), 0)
    col = _iota((BAND, 2 * BAND), 1)
    mask_win = (col >= row) & (col <= row + BAND)
    mask_own = _iota((BAND, BAND), 1) <= _iota((BAND, BAND), 0)

    srcs = ((q1, k1, v1), (q2, k2, v2), (q3, k3, v3))
    slot = {}
    for g, (_, dil) in enumerate(DIL_GROUPS):
        if dil > 1:
            for a in range(3):
                slot[g, a] = len(slot)
                f_ref[slot[g, a]] = _f32(srcs[g][a][...])

    units = []
    for g, (_, dil) in enumerate(DIL_GROUPS):
        sub_len = seq // dil
        for r in range(dil):
            if dil > 1:
                qs, ks, vs = (_bf16(f_ref[slot[g, a], pl.ds(r, sub_len, stride=dil), :]) for a in range(3))
            else:
                qs, ks, vs = (srcs[g][a][...] for a in range(3))
            qhs = _split_heads(qs)
            for qb in range(sub_len // BAND):
                lo, hi = max(qb - 1, 0) * BAND, (qb + 1) * BAND
                for h in range(2):
                    units.append((g, r, qb, h, qhs[h][qb * BAND:hi], ks[lo:hi], vs[lo:hi],
                                  mask_own if qb == 0 else mask_win))

    scores, done_units = {}, {}
    for step in range(len(units) + DIL_AHEAD):
        if step < len(units):
            _, _, _, _, qh, k, _, mask = units[step]
            scores[step] = jnp.where(mask, _dot_nt(qh, k), NEG)
        done = step - DIL_AHEAD
        if done >= 0:
            g, r, qb, h, _, _, v, _ = units[done]
            done_units[g, r, qb, h] = _band_finish(scores.pop(done), v)
    for g, (_, dil) in enumerate(DIL_GROUPS):
        sub_len = seq // dil
        for r in range(dil):
            pairs = [(done_units[g, r, qb, 0], done_units[g, r, qb, 1]) for qb in range(sub_len // BAND)]
            og_ref[g, pl.ds(r, sub_len, stride=dil), :] = jnp.concatenate(
                [jnp.where(head_lo, a[0], b[0]) for a, b in pairs], axis=0)
            lg_ref[g, pl.ds(r, sub_len, stride=dil), :] = jnp.concatenate(
                [jnp.where(head_lo, a[1], b[1]) for a, b in pairs], axis=0)

    l1, l2, l3 = lg_ref[0], lg_ref[1], lg_ref[2]
    mx = jnp.maximum(jnp.maximum(l1, l2), l3)
    w1, w2, w3 = jnp.exp2(l1 - mx), jnp.exp2(l2 - mx), jnp.exp2(l3 - mx)
    o = (w1 * og_ref[0] + w2 * og_ref[1] + w3 * og_ref[2]) / (w1 + w2 + w3)
    o_ref[...] = _bf16(o)


def _dil_call(qkv3):
    b, seq, _ = qkv3.shape
    n_slot_pairs = HEADS_PER_DIL // 2

    def spec(base, g):
        return pl.BlockSpec((None, seq, LANES), lambda bi, p: (bi, 0, base + g * n_slot_pairs + p))

    in_specs = []
    for g in range(len(DIL_GROUPS)):
        in_specs += [spec(QR0, g), spec(KR0, g), spec(V0, g)]
    return pl.pallas_call(
        functools.partial(_dil_body, seq=seq),
        grid=(b, n_slot_pairs),
        in_specs=in_specs,
        out_specs=pl.BlockSpec((None, seq, LANES), lambda bi, p: (bi, 0, p)),
        out_shape=jax.ShapeDtypeStruct((b, seq, n_slot_pairs * LANES), jnp.bfloat16),
        scratch_shapes=[
            pltpu.VMEM((3 * sum(dil > 1 for _, dil in DIL_GROUPS), seq, LANES), jnp.float32),
            pltpu.VMEM((len(DIL_GROUPS), seq, LANES), jnp.float32),
            pltpu.VMEM((len(DIL_GROUPS), seq, LANES), jnp.float32),
        ],
        compiler_params=pltpu.CompilerParams(
            dimension_semantics=("arbitrary", "arbitrary"), vmem_limit_bytes=VMEM_LIMIT),
        name="dilated_attn",
    )(*([qkv3] * 9))


def _moba_prepare(q_ref, k_ref, v_ref, kmean_ref, qa_ref, ka_ref, va_ref, n_blk, first):
    v = v_ref[...]
    v_lo = _iota(v.shape, 1) < HEAD_DIM
    one = jnp.ones_like(v)
    va_ref[0] = jnp.where(v_lo, v, one)
    va_ref[1] = jnp.where(v_lo, one, v)
    if first >= n_blk:
        return

    kmean_ref[...] = jnp.zeros_like(kmean_ref)
    for j in range(n_blk):
        kj = _f32(k_ref[j * BLK:(j + 1) * BLK, :])
        kmean_ref[j:j + 1, :] = jnp.mean(kj, axis=0, keepdims=True)
    km = kmean_ref[...]
    km_hi = _bf16(km)
    km_mid = _bf16(km - _f32(km_hi))
    km_lo = _bf16(km - _f32(km_hi) - _f32(km_mid))

    q = q_ref[first * BLK:, :]
    k = k_ref[...]
    nq = q.shape[0]
    q_lo = _qk_head_lo(q.shape)
    k_lo = _qk_head_lo(k.shape)
    lane = _iota(k.shape, 1)
    key_blk = _iota(k.shape, 0) // BLK
    blk_id = _iota((KMEAN_ROWS, nq), 0)
    past = blk_id < _iota((KMEAN_ROWS, nq), 1) // BLK + first
    zero = jnp.zeros_like(q)
    pad = jnp.zeros((LANES - KMEAN_ROWS, nq), jnp.float32)
    for h in range(2):
        own = q_lo if h == 0 else ~q_lo
        qh = jnp.where(own, q, zero)
        gate = _dot_nt(km_hi, qh) + _dot_nt(km_mid, qh) + _dot_nt(km_lo, qh)
        sel = jnp.zeros(gate.shape, jnp.float32)
        for j in range(n_blk - 1):
            gj = gate[j:j + 1, :]
            beats = ((gate > gj) | ((gate == gj) & (blk_id < j))) & past
            rank = jnp.sum(jnp.where(beats, 1.0, 0.0), axis=0, keepdims=True)
            sel = jnp.where((blk_id == j) & (rank < MOBA_TOPK), 1.0, sel)
        bias = jnp.where(past & (sel < 0.5), NEG, 0.0)
        bias_t = jnp.concatenate([bias, pad], axis=0).T
        base = (1 - h) * (HEAD_DIM // 2)
        if base:
            bias_t = pltpu.roll(bias_t, base, 1)
        qa_ref[h] = jnp.where(own, q, _bf16(bias_t))
        ka_ref[h] = jnp.where(k_lo if h == 0 else ~k_lo, k,
                              jnp.where(lane - base == key_blk, 1.0, 0.0).astype(k.dtype))


def _moba_scores(c, h, first, q_ref, k_ref, qa_ref, ka_ref):
    n = (c + 1) * BLK
    if c < first:
        return _dot_nt(_split_heads(q_ref[c * BLK:n, :])[h], k_ref[0:n, :])
    return _dot_nt(qa_ref[h, (c - first) * BLK:(c - first + 1) * BLK, :], ka_ref[h, 0:n, :])


def _moba_finish(c, h, s, va_ref):
    n = (c + 1) * BLK
    causal = _iota((BLK, BLK), 1) <= _iota((BLK, BLK), 0)
    own = jnp.where(causal, s[:, c * BLK:], NEG)
    m = jnp.max(own, axis=-1, keepdims=True)
    if c > 0:
        left = s[:, :c * BLK]
        m = jnp.maximum(m, jnp.max(left, axis=-1, keepdims=True))
        p = jnp.concatenate([_bf16(jnp.exp2(left - m)), _bf16(jnp.exp2(own - m))], axis=1)
    else:
        p = _bf16(jnp.exp2(own - m))
    acc = _dot(p, va_ref[h, 0:n, :])
    return acc / pltpu.roll(acc, HEAD_DIM, 1)


def _moba_body(q_ref, k_ref, v_ref, o_ref, kmean_ref, qa_ref, ka_ref, va_ref, *, n_blk, first):
    _moba_prepare(q_ref, k_ref, v_ref, kmean_ref, qa_ref, ka_ref, va_ref, n_blk, first)
    units = [(c, h) for c in range(n_blk) for h in range(2)]
    outs, scores = {}, {}
    for step in range(len(units) + MOBA_AHEAD):
        if step < len(units):
            scores[step] = _moba_scores(*units[step], first, q_ref, k_ref, qa_ref, ka_ref)
        done = step - MOBA_AHEAD
        if done >= 0:
            outs[units[done]] = _moba_finish(*units[done], scores.pop(done), va_ref)
    head_lo = _iota((BLK, LANES), 1) < HEAD_DIM
    for c in range(n_blk):
        o_ref[c * BLK:(c + 1) * BLK, :] = _bf16(jnp.where(head_lo, outs[c, 0], outs[c, 1]))


def _moba_call(qkv3):
    b, seq, _ = qkv3.shape
    n_blk = seq // BLK
    first = min(MOBA_TOPK + 1, n_blk)
    n_sel = max(seq - first * BLK, BLK)
    return pl.pallas_call(
        functools.partial(_moba_body, n_blk=n_blk, first=first),
        grid=(b, NPB),
        in_specs=[
            pl.BlockSpec((None, seq, LANES), lambda bi, p: (bi, 0, QR0 + NPA + p)),
            pl.BlockSpec((None, seq, LANES), lambda bi, p: (bi, 0, KR0 + NPA + p)),
            pl.BlockSpec((None, seq, LANES), lambda bi, p: (bi, 0, V0 + NPA + p)),
        ],
        out_specs=pl.BlockSpec((None, seq, LANES), lambda bi, p: (bi, 0, p)),
        out_shape=jax.ShapeDtypeStruct((b, seq, NPB * LANES), jnp.bfloat16),
        scratch_shapes=[pltpu.VMEM((KMEAN_ROWS, LANES), jnp.float32),
                        pltpu.VMEM((2, n_sel, LANES), jnp.bfloat16),
                        pltpu.VMEM((2, seq, LANES), jnp.bfloat16),
                        pltpu.VMEM((2, seq, LANES), jnp.bfloat16)],
        compiler_params=pltpu.CompilerParams(
            dimension_semantics=("arbitrary", "arbitrary"), vmem_limit_bytes=VMEM_LIMIT),
        name="moba_attn",
    )(qkv3, qkv3, qkv3)


def _sb_tile(c, h, j, state, q_ref, k_ref, v_ref):
    row = _iota((BLK, BLK), 0)
    col = _iota((BLK, BLK), 1)
    past = col < row
    from_s = jnp.where(row >= col, 1.0, 0.0).astype(jnp.bfloat16)
    from_s2 = jnp.concatenate([from_s, from_s], axis=0)

    qh = _split_heads(q_ref[c * BLK:(c + 1) * BLK, :])[h]
    z = _dot_nt(qh, k_ref[j * BLK:(j + 1) * BLK, :])
    yield

    sp = jnp.maximum(z, 0.0) + _log2(1.0 + jnp.exp2(-jnp.abs(z)))
    if j == c:
        sp = jnp.where(past, sp, 0.0)
    hi = _bf16(sp)
    lo = _bf16(sp - _f32(hi))
    tail = _dot(jnp.concatenate([hi, lo], axis=1), from_s2)
    beyond = state.get("beyond")
    total = tail[:, 0:1]
    state["beyond"] = total if beyond is None else beyond + total
    yield

    t = z - tail
    if beyond is not None:
        t = t - beyond
    a = jnp.exp2(t)
    if j == c:
        a = jnp.where(past, a, 0.0)
    part = _dot(_bf16(a), v_ref[j * BLK:(j + 1) * BLK, :])
    state["acc"] = part if "acc" not in state else state["acc"] + part
    yield


def _sb_body(q_ref, k_ref, v_ref, o_ref, *, n_blk):
    states = {(c, h): {} for c in range(n_blk) for h in range(2)}
    tiles = [(c, h, j) for c in range(n_blk) for h in range(2) for j in range(c, -1, -1)]
    gens = [_sb_tile(c, h, j, states[c, h], q_ref, k_ref, v_ref) for c, h, j in tiles]
    n_stages = 3
    for step in range(len(tiles) + (n_stages - 1) * SB_SKEW):
        for stage in range(n_stages):
            t = step - stage * SB_SKEW
            if 0 <= t < len(tiles):
                next(gens[t])
    head_lo = _iota((BLK, LANES), 1) < HEAD_DIM
    for c in range(n_blk):
        o_ref[c * BLK:(c + 1) * BLK, :] = _bf16(jnp.where(head_lo, states[c, 0]["acc"], states[c, 1]["acc"]))


def _sb_call(qkv3):
    b, seq, _ = qkv3.shape
    n_blk = seq // BLK
    return pl.pallas_call(
        functools.partial(_sb_body, n_blk=n_blk),
        grid=(b, NPC),
        in_specs=[
            pl.BlockSpec((None, seq, LANES), lambda bi, p: (bi, 0, QC0 + p)),
            pl.BlockSpec((None, seq, LANES), lambda bi, p: (bi, 0, KC0 + p)),
            pl.BlockSpec((None, seq, LANES), lambda bi, p: (bi, 0, V0 + NPA + NPB + p)),
        ],
        out_specs=pl.BlockSpec((None, seq, LANES), lambda bi, p: (bi, 0, p)),
        out_shape=jax.ShapeDtypeStruct((b, seq, NPC * LANES), jnp.bfloat16),
        compiler_params=pltpu.CompilerParams(
            dimension_semantics=("arbitrary", "arbitrary"), vmem_limit_bytes=VMEM_LIMIT),
        name="stickbreak_attn",
    )(qkv3, qkv3, qkv3)


def _merge_body(x_ref, mod_ref, g_ref, oa_ref, ob_ref, oc_ref, wa32_ref, wb32_ref, wc32_ref,
                wg32_ref, bg_ref, wo32_ref, out_ref, wa_ref, wb_ref, wc_ref, wg_ref, wo_ref):
    @pl.when(pl.program_id(0) == 0)
    def _():
        for src, dst in ((wa32_ref, wa_ref), (wb32_ref, wb_ref), (wc32_ref, wc_ref),
                         (wg32_ref, wg_ref), (wo32_ref, wo_ref)):
            dst[...] = _bf16(src[...])

    x = x_ref[...]
    d = x.shape[1]
    h = _bf16(_modulated_norm(x, g_ref[...], mod_ref[0:1, :], mod_ref[1:2, :]))
    merged = None
    for br, (o_ref, w_ref) in enumerate(((oa_ref, wa_ref), (ob_ref, wb_ref), (oc_ref, wc_ref))):
        gate = jax.nn.sigmoid(_dot(h, wg_ref[:, br * d:(br + 1) * d]) + bg_ref[:, br * d:(br + 1) * d])
        term = gate * _dot(o_ref[...], w_ref[...])
        merged = term if merged is None else merged + term
    out_ref[...] = x + mod_ref[2:3, :] * _dot(_bf16(merged), wo_ref[...])


def _merge_call(x2, mod, norm_g, oa, ob, oc, w_br_a, w_br_b, w_br_c, w_gate, b_gate, w_out, layer, seq):
    t, d = x2.shape
    tm = ROW_TILE
    tiles_per_seq = seq // tm

    def whole(arr):
        return pl.BlockSpec((None,) + arr.shape[1:], lambda i: (layer,) + (0,) * (arr.ndim - 1),
                            pipeline_mode=pl.Buffered(1))

    def rows(arr):
        return pl.BlockSpec((tm, arr.shape[1]), lambda i: (i, 0))

    return pl.pallas_call(
        _merge_body,
        grid=(t // tm,),
        in_specs=[
            rows(x2),
            pl.BlockSpec((None, None, 6, d), lambda i: (layer, i // tiles_per_seq, 0, 0)),
            whole(norm_g), rows(oa), rows(ob), rows(oc),
            whole(w_br_a), whole(w_br_b), whole(w_br_c), whole(w_gate), whole(b_gate), whole(w_out),
        ],
        out_specs=rows(x2),
        out_shape=jax.ShapeDtypeStruct((t, d), jnp.float32),
        scratch_shapes=[pltpu.VMEM(w.shape[1:], jnp.bfloat16)
                        for w in (w_br_a, w_br_b, w_br_c, w_gate, w_out)],
        compiler_params=pltpu.CompilerParams(
            dimension_semantics=("arbitrary",), vmem_limit_bytes=VMEM_LIMIT),
        name="gated_merge",
    )(x2, mod, norm_g, oa, ob, oc, w_br_a, w_br_b, w_br_c, w_gate, b_gate, w_out)


def _ffn_body(x_ref, mod_ref, g_ref, wg_ref, wu_ref, wd_ref, *rest):
    maybe_final_g_ref, out_ref = rest[:-1], rest[-1]
    x = x_ref[...]
    h = _bf16(_modulated_norm(x, g_ref[...], mod_ref[3:4, :], mod_ref[4:5, :]))
    gt = _dot(h, wg_ref[...])
    up = _dot(h, wu_ref[...])
    act = (gt * jax.nn.sigmoid(gt)) * up
    y = x + mod_ref[5:6, :] * _dot(_bf16(act), wd_ref[...])
    for final_g_ref in maybe_final_g_ref:
        y = _rms_norm(y, final_g_ref[...])
    out_ref[...] = y


def _ffn_call(x2, mod, norm_g, w_gu, w_down, layer, seq, final_g=None):
    t, d = x2.shape
    d_ff = w_down.shape[1]
    tm = ROW_TILE
    tiles_per_seq = seq // tm
    resident = pl.Buffered(1)
    in_specs = [
        pl.BlockSpec((tm, d), lambda i: (i, 0)),
        pl.BlockSpec((None, None, 6, d), lambda i: (layer, i // tiles_per_seq, 0, 0)),
        pl.BlockSpec((None, 1, d), lambda i: (layer, 0, 0)),
        pl.BlockSpec((None, d, d_ff), lambda i: (layer, 0, 0), pipeline_mode=resident),
        pl.BlockSpec((None, d, d_ff), lambda i: (layer, 0, 1), pipeline_mode=resident),
        pl.BlockSpec((None, d_ff, d), lambda i: (layer, 0, 0), pipeline_mode=resident),
    ]
    args = [x2, mod, norm_g, w_gu, w_gu, w_down]
    if final_g is not None:
        in_specs.append(pl.BlockSpec((1, d), lambda i: (0, 0)))
        args.append(final_g.reshape(1, d))
    return pl.pallas_call(
        _ffn_body,
        grid=(t // tm,),
        in_specs=in_specs,
        out_specs=pl.BlockSpec((tm, d), lambda i: (i, 0)),
        out_shape=jax.ShapeDtypeStruct((t, d), jnp.float32),
        compiler_params=pltpu.CompilerParams(
            dimension_semantics=("arbitrary",), vmem_limit_bytes=VMEM_LIMIT),
        name="swiglu_ffn",
    )(*args)


def _rope_tables(seq):
    pos = jnp.arange(seq, dtype=jnp.float32)
    inv = ROPE_THETA ** (-jnp.arange(0, HEAD_DIM, 2, dtype=jnp.float32) / HEAD_DIM)
    ang = pos[:, None] * inv[None, :]
    cos, sin = jnp.cos(ang), jnp.sin(ang)
    cos_t = jnp.tile(cos, (1, 2 * LANES // HEAD_DIM))
    sin_t = jnp.concatenate([-sin, -sin, sin, sin], axis=-1)
    return cos_t, sin_t


def kernel(x, c, w_ada, b_ada, norm1_g, w_in, w_br_a, w_br_b, w_br_c, w_gate, b_gate, w_out,
           norm2_g, w_gu, w_down, final_g):
    b, seq, d = x.shape
    depth = w_in.shape[0]
    cos_t, sin_t = _rope_tables(seq)
    w_gu, w_down = _bf16(w_gu), _bf16(w_down)
    b_gate3 = b_gate.reshape(depth, 1, -1)
    g1 = norm1_g.reshape(depth, 1, d)
    g2 = norm2_g.reshape(depth, 1, d)

    mod = _ada_call(c, w_ada, b_ada).reshape(depth, b, 6, d)
    x2 = x.reshape(b * seq, d)
    for layer in range(depth):
        qkv = _qkv_call(x2, mod, g1, w_in, cos_t, sin_t, layer, seq)
        qkv3 = qkv.reshape(b, seq, -1)
        oa = _dil_call(qkv3).reshape(b * seq, -1)
        ob = _moba_call(qkv3).reshape(b * seq, -1)
        oc = _sb_call(qkv3).reshape(b * seq, -1)
        x2 = _merge_call(x2, mod, g1, oa, ob, oc, w_br_a, w_br_b, w_br_c, w_gate, b_gate3, w_out,
                         layer, seq)
        x2 = _ffn_call(x2, mod, g2, w_gu, w_down, layer, seq,
                       final_g=final_g if layer == depth - 1 else None)
    return x2.reshape(b, seq, d)
```
